```python
import jax, jax.numpy as jnp
from jax import lax
import numpy as np

D_MODEL = 2048
BATCH = 4
SEQ = 4096
DEPTH = 4

CHUNK = 64
PLE_DIM = 256
A_HEADS = 8
A_HEAD_DIM = 128
A_WIDTH = A_HEADS * A_HEAD_DIM
B_HEADS = 8
B_HEAD_DIM = 128
B_WIDTH = B_HEADS * B_HEAD_DIM
IDX_HEADS = 16
IDX_DIM = 64
TOPK_MAX = 256
QBLOCK = 128
ROPE_THETA = 10000.0
D_FF = ((8 * D_MODEL // 3 + 255) // 256) * 256
ALPHA = (2 * DEPTH) ** 0.25
BETA = (8 * DEPTH) ** -0.25
LN_EPS = 1e-5
RMS_EPS = 1e-6
MASK_VALUE = -1e30
ATTN_SCALE = B_HEAD_DIM ** -0.5
IDX_SCALE = (IDX_HEADS * IDX_DIM) ** -0.5
IN_SPLITS = (A_WIDTH, A_WIDTH, A_WIDTH, A_WIDTH,
             B_WIDTH, B_WIDTH, B_WIDTH,
             IDX_HEADS * IDX_DIM, IDX_DIM, IDX_HEADS,
             D_MODEL, D_MODEL)
IN_WIDTH = sum(IN_SPLITS)

kernel_name = "hgrn2_dsa_gated_hybrid_deepnorm"


def layer_norm(x, g, b):
    xf = x.astype(jnp.float32)
    mu = jnp.mean(xf, axis=-1, keepdims=True)
    xc = xf - mu
    var = jnp.mean(xc * xc, axis=-1, keepdims=True)
    return (xc * lax.rsqrt(var + LN_EPS) * g.astype(jnp.float32) + b.astype(jnp.float32)).astype(x.dtype)


def rope_tables(positions, dim):
    half = dim // 2
    inv_freq = ROPE_THETA ** (-jnp.arange(half, dtype=jnp.float32) / half)
    ang = positions.astype(jnp.float32)[..., None] * inv_freq
    return jnp.cos(ang), jnp.sin(ang)


def apply_rope(x, cos, sin):
    half = x.shape[-1] // 2
    xf = x.astype(jnp.float32)
    x1, x2 = xf[..., :half], xf[..., half:]
    return jnp.concatenate([x1 * cos - x2 * sin, x1 * sin + x2 * cos], axis=-1).astype(x.dtype)


def hgrn2_mixer(q, f_pre, i_in, g, lb, norm_g):
    f32 = jnp.float32
    bsz, seq, _ = q.shape
    n_chunks = seq // CHUNK
    f_pre = f_pre.astype(f32)
    lb = lb.astype(f32)
    log_f = jax.nn.log_sigmoid(f_pre) + jnp.log1p(lb * jnp.exp(-f_pre))
    k = (1.0 - lb) * jax.nn.sigmoid(-f_pre)
    qf = jax.nn.silu(q.astype(f32))

    def to_chunks(t):
        return t.reshape(bsz, n_chunks, CHUNK, A_HEADS, A_HEAD_DIM).transpose(1, 0, 3, 2, 4)

    xs = (to_chunks(qf), to_chunks(k), to_chunks(i_in.astype(f32)), to_chunks(log_f))
    causal = jnp.tril(jnp.ones((CHUNK, CHUNK), dtype=bool))

    def step(state, inp):
        qc, kc, vc, gc = inp
        cum = jnp.cumsum(gc, axis=2)
        o_inter = jnp.einsum('bhtk,bhkv->bhtv', qc * jnp.exp(cum), state)
        diff = cum[:, :, :, None, :] - cum[:, :, None, :, :]
        decay = jnp.exp(jnp.where(causal[:, :, None], diff, MASK_VALUE))
        scores = jnp.einsum('bhtk,bhsk,bhtsk->bhts', qc, kc, decay)
        o = o_inter + jnp.einsum('bhts,bhsv->bhtv', scores, vc)
        last = cum[:, :, -1:, :]
        new_state = jnp.exp(last[:, :, 0, :])[..., None] * state + \
            jnp.einsum('bhsk,bhsv->bhkv', kc * jnp.exp(last - cum), vc)
        return new_state, o

    state0 = jnp.zeros((bsz, A_HEADS, A_HEAD_DIM, A_HEAD_DIM), f32)
    _, o = lax.scan(step, state0, xs)
    o = o.transpose(1, 0, 3, 2, 4).reshape(bsz, seq, A_HEADS, A_HEAD_DIM)
    o = o * lax.rsqrt(jnp.mean(o * o, axis=-1, keepdims=True) + RMS_EPS) * norm_g.astype(f32)
    gate = jax.nn.silu(g.astype(f32)).reshape(bsz, seq, A_HEADS, A_HEAD_DIM)
    return (o * gate).reshape(bsz, seq, A_WIDTH).astype(q.dtype)


def dsa_mixer(q, k, v, q_idx, k_idx, w_idx, cos_h, sin_h, cos_i, sin_i, kn_g, kn_b):
    f32 = jnp.float32
    bsz, seq, _ = q.shape
    topk = min(TOPK_MAX, seq // 4)
    n_blocks = seq // QBLOCK
    q = apply_rope(q.reshape(bsz, seq, B_HEADS, B_HEAD_DIM), cos_h[:, :, None], sin_h[:, :, None])
    k = apply_rope(k.reshape(bsz, seq, B_HEADS, B_HEAD_DIM), cos_h[:, :, None], sin_h[:, :, None])
    v = v.reshape(bsz, seq, B_HEADS, B_HEAD_DIM)
    q_idx = apply_rope(q_idx.reshape(bsz, seq, IDX_HEADS, IDX_DIM),
                       cos_i[:, :, None], sin_i[:, :, None]).astype(f32)
    k_idx = apply_rope(layer_norm(k_idx, kn_g, kn_b), cos_i, sin_i).astype(f32)
    w_idx = w_idx.astype(f32) * IDX_SCALE
    key_chunk = jnp.arange(seq) // CHUNK
    gather = jax.vmap(lambda t, idx: t[idx])

    def to_blocks(t):
        return t.reshape(bsz, n_blocks, QBLOCK, *t.shape[2:]).swapaxes(0, 1)

    def one_block(blk):
        qb, qib, wib, start = blk
        q_chunk = (start + jnp.arange(QBLOCK)) // CHUNK
        rel = jax.nn.relu(jnp.einsum('bqhd,bsd->bqhs', qib, k_idx))
        score = jnp.einsum('bqh,bqhs->bqs', wib, rel)
        score = jnp.where(key_chunk[None, None, :] <= q_chunk[None, :, None], score, MASK_VALUE)
        _, sel = lax.top_k(score, topk)
        ks = gather(k, sel)
        vs = gather(v, sel)
        valid = key_chunk[sel] <= q_chunk[None, :, None]
        logits = jnp.einsum('bqhd,bqkhd->bqhk', qb, ks).astype(f32) * ATTN_SCALE
        logits = jnp.where(valid[:, :, None, :], logits, MASK_VALUE)
        probs = jax.nn.softmax(logits, axis=-1).astype(vs.dtype)
        return jnp.einsum('bqhk,bqkhd->bqhd', probs, vs)

    starts = jnp.arange(n_blocks, dtype=jnp.int32) * QBLOCK
    out = lax.map(one_block, (to_blocks(q), to_blocks(q_idx), to_blocks(w_idx), starts))
    return out.swapaxes(0, 1).reshape(bsz, seq, B_WIDTH)


def setup_inputs(seed: int = 0) -> dict:
    key = jax.random.key(seed)
    ks = jax.random.split(key, 24)
    f32 = jnp.float32

    def normal(k, shape, scale):
        return jax.random.normal(k, shape, f32) * scale

    def gain(k, shape):
        return 1.0 + 0.01 * jax.random.normal(k, shape, f32)

    def bias(k, shape):
        return 0.01 * jax.random.normal(k, shape, f32)

    x = normal(ks[0], (BATCH, SEQ, D_MODEL), 1.0)
    p = normal(ks[1], (DEPTH, BATCH, SEQ, PLE_DIM), 1.0)
    offsets = jax.random.randint(ks[2], (BATCH, 1), 0, 64, dtype=jnp.int32) * CHUNK
    positions = (offsets + jnp.arange(SEQ, dtype=jnp.int32)[None, :]).astype(jnp.int32)
    return {
        "x": x,
        "p": p,
        "positions": positions,
        "w_in": normal(ks[3], (DEPTH, D_MODEL, IN_WIDTH), D_MODEL ** -0.5),
        "w_branch_a": normal(ks[4], (DEPTH, A_WIDTH, D_MODEL), A_WIDTH ** -0.5),
        "w_branch_b": normal(ks[5], (DEPTH, B_WIDTH, D_MODEL), B_WIDTH ** -0.5),
        "w_out": normal(ks[6], (DEPTH, D_MODEL, D_MODEL), BETA * D_MODEL ** -0.5),
        "hgrn_lower_bounds": normal(ks[7], (DEPTH, A_WIDTH), 0.1),
        "hgrn_norm_g": gain(ks[8], (DEPTH, A_HEAD_DIM)),
        "idx_k_norm_g": gain(ks[9], (DEPTH, IDX_DIM)),
        "idx_k_norm_b": bias(ks[10], (DEPTH, IDX_DIM)),
        "ln_mix_g": gain(ks[11], (DEPTH, D_MODEL)),
        "ln_mix_b": bias(ks[12], (DEPTH, D_MODEL)),
        "w_ffn_gate": normal(ks[13], (DEPTH, D_MODEL, D_FF), D_MODEL ** -0.5),
        "w_ffn_up": normal(ks[14], (DEPTH, D_MODEL, D_FF), D_MODEL ** -0.5),
        "w_ffn_down": normal(ks[15], (DEPTH, D_FF, D_MODEL), BETA * D_FF ** -0.5),
        "ln_ffn_g": gain(ks[16], (DEPTH, D_MODEL)),
        "ln_ffn_b": bias(ks[17], (DEPTH, D_MODEL)),
        "w_ple_gate": normal(ks[18], (DEPTH, D_MODEL, D_MODEL), D_MODEL ** -0.5),
        "w_ple_proj": normal(ks[19], (DEPTH, PLE_DIM, D_MODEL), BETA * PLE_DIM ** -0.5),
        "ln_ple_g": gain(ks[20], (DEPTH, D_MODEL)),
        "ln_ple_b": bias(ks[21], (DEPTH, D_MODEL)),
    }


def reference(x, p, positions, w_in, w_branch_a, w_branch_b, w_out, hgrn_lower_bounds,
              hgrn_norm_g, idx_k_norm_g, idx_k_norm_b, ln_mix_g, ln_mix_b,
              w_ffn_gate, w_ffn_up, w_ffn_down, ln_ffn_g, ln_ffn_b,
              w_ple_gate, w_ple_proj, ln_ple_g, ln_ple_b):
    cos_h, sin_h = rope_tables(positions, B_HEAD_DIM)
    cos_i, sin_i = rope_tables(positions, IDX_DIM)
    lb_sm = jax.nn.softmax(hgrn_lower_bounds.astype(jnp.float32), axis=0)
    lower_bounds = jnp.cumsum(lb_sm, axis=0) - lb_sm[0]
    split_at = np.cumsum(IN_SPLITS)[:-1].tolist()
    for layer in range(DEPTH):
        (a_q, a_f, a_i, a_g, b_q, b_k, b_v, i_q, i_k, i_w, gate_a, gate_b) = \
            jnp.split(x @ w_in[layer], split_at, axis=-1)
        y_a = hgrn2_mixer(a_q, a_f, a_i, a_g, lower_bounds[layer], hgrn_norm_g[layer])
        y_b = dsa_mixer(b_q, b_k, b_v, i_q, i_k, i_w, cos_h, sin_h, cos_i, sin_i,
                        idx_k_norm_g[layer], idx_k_norm_b[layer])
        merged = jax.nn.sigmoid(gate_a) * (y_a @ w_branch_a[layer]) + \
            jax.nn.sigmoid(gate_b) * (y_b @ w_branch_b[layer])
        x = layer_norm(ALPHA * x + merged @ w_out[layer], ln_mix_g[layer], ln_mix_b[layer])
        ffn = (jax.nn.silu(x @ w_ffn_gate[layer]) * (x @ w_ffn_up[layer])) @ w_ffn_down[layer]
        x = layer_norm(ALPHA * x + ffn, ln_ffn_g[layer], ln_ffn_b[layer])
        ple = jax.nn.sigmoid(x @ w_ple_gate[layer]) * (p[layer] @ w_ple_proj[layer])
        x = layer_norm(ALPHA * x + ple, ln_ple_g[layer], ln_ple_b[layer])
    return x
```

```python
import functools

import numpy as np
import jax
import jax.numpy as jnp
from jax import lax
from jax.experimental import pallas as pl
from jax.experimental.pallas import tpu as pltpu

F32 = jnp.float32
I32 = jnp.int32
MXU_DTYPE = jnp.bfloat16

CHUNK = 64
A_HEADS = 8
A_HEAD_DIM = 128
B_HEADS = 8
B_HEAD_DIM = 128
IDX_HEADS = 16
IDX_DIM = 64
TOPK_MAX = 256
ROPE_THETA = 10000.0
LN_EPS = 1e-5
RMS_EPS = 1e-6
MASK_VALUE = -1e30
ATTN_SCALE = B_HEAD_DIM ** -0.5
IDX_SCALE = (IDX_HEADS * IDX_DIM) ** -0.5
A_WIDTH = A_HEADS * A_HEAD_DIM
B_WIDTH = B_HEADS * B_HEAD_DIM
IDX_WIDTH = IDX_HEADS * IDX_DIM
LANES = 128
N_LEVELS = 6
VMEM_LIMIT = 52 * 1024 * 1024

_INT_MIN = -2 ** 31


def _float_key(v):
    bits = int(np.float32(v).view(np.int32))
    return bits ^ ((bits >> 31) & 0x7FFFFFFF)


KEY_MASK = _float_key(MASK_VALUE)


def _dot(a, b):
    return jnp.dot(a, b, preferred_element_type=F32)


def _dot_nt(a, b):
    return lax.dot_general(a, b, (((1,), (1,)), ((), ())), preferred_element_type=F32)


def _dot_tn(a, b):
    return lax.dot_general(a, b, (((0,), (0,)), ((), ())), preferred_element_type=F32)


def _sigmoid(x):
    return 1.0 / (1.0 + jnp.exp(-x))


def _silu(x):
    return x * _sigmoid(x)


def _layer_norm(z, g, b):
    mu = jnp.mean(z, axis=-1, keepdims=True)
    zc = z - mu
    var = jnp.mean(zc * zc, axis=-1, keepdims=True)
    return zc * lax.rsqrt(var + LN_EPS) * g + b


def _params(*sem):
    return pltpu.CompilerParams(dimension_semantics=sem, vmem_limit_bytes=VMEM_LIMIT)


def _resident(block_shape, index_map):
    return pl.BlockSpec(block_shape, index_map, pipeline_mode=pl.Buffered(1))


def _mm_kernel(x_ref, w_ref, o_ref, *, act):
    acc = _dot(x_ref[...], w_ref[...])
    if act == "sigmoid":
        acc = _sigmoid(acc)
    o_ref[...] = acc.astype(o_ref.dtype)


def _matmul(xb, w, layer, col_blk0, ncols, *, tm, tn, out_dtype, act=None):
    m, k = xb.shape
    return pl.pallas_call(
        functools.partial(_mm_kernel, act=act),
        name="proj_" + (act or "linear"),
        grid=(ncols // tn, m // tm),
        in_specs=[pl.BlockSpec((tm, k), lambda j, i: (i, 0)),
                  pl.BlockSpec((None, k, tn), lambda j, i: (layer, 0, col_blk0 + j))],
        out_specs=pl.BlockSpec((tm, tn), lambda j, i: (i, j)),
        out_shape=jax.ShapeDtypeStruct((m, ncols), out_dtype),
        compiler_params=_params("arbitrary", "arbitrary"),
    )(xb, w)


def _rope_kernel(pos_ref, freq_ref, sign_ref, cos_ref, sin_ref):
    ang = pos_ref[...].astype(F32) * freq_ref[...]
    cos_ref[...] = jnp.cos(ang)
    sin_ref[...] = jnp.sin(ang) * sign_ref[...]


def _rope_tables(positions, ts):
    m = positions.size
    half_h, half_i = B_HEAD_DIM // 2, IDX_DIM // 2
    lane = np.arange(LANES)
    inv_h = ROPE_THETA ** (-jnp.arange(half_h, dtype=F32) / half_h)
    inv_i = ROPE_THETA ** (-jnp.arange(half_i, dtype=F32) / half_i)
    freq = jnp.concatenate([jnp.tile(inv_h, LANES // half_h), jnp.tile(inv_i, LANES // half_i)])[None, :]
    s_h = np.where(lane % B_HEAD_DIM < half_h, -1.0, 1.0)
    s_i = np.where(lane % IDX_DIM < half_i, -1.0, 1.0)
    sign = jnp.asarray(np.concatenate([s_h, s_i])[None, :], F32)
    return pl.pallas_call(
        _rope_kernel,
        name="rope_tables",
        grid=(m // ts,),
        in_specs=[pl.BlockSpec((ts, 1), lambda i: (i, 0)),
                  pl.BlockSpec((1, 2 * LANES), lambda i: (0, 0)),
                  pl.BlockSpec((1, 2 * LANES), lambda i: (0, 0))],
        out_specs=[pl.BlockSpec((ts, 2 * LANES), lambda i: (i, 0))] * 2,
        out_shape=[jax.ShapeDtypeStruct((m, 2 * LANES), F32)] * 2,
        compiler_params=_params("arbitrary"),
    )(positions.reshape(m, 1), freq, sign)


def _hgrn_tables():
    c = CHUNK
    mall = np.zeros((N_LEVELS + 2, c, c), np.float32)
    masks = np.zeros((N_LEVELS + 1, c, c), np.float32)
    t = np.arange(c)
    mall[0] = (t[None, :] <= t[:, None])
    masks[0] = np.eye(c)
    for lvl in range(1, N_LEVELS + 1):
        half = c >> lvl
        mid = (t // (2 * half)) * (2 * half) + half - 1
        right = t > mid
        j = t[None, :]
        mall[lvl] = np.where(right[:, None], (j > mid[:, None]) & (j <= t[:, None]),
                             (j > t[:, None]) & (j <= mid[:, None]))
        same = (t[:, None] // (2 * half)) == (t[None, :] // (2 * half))
        masks[lvl] = same & right[:, None] & (~right[None, :])
    mall[N_LEVELS + 1] = (t[None, :] > t[:, None])
    return (jnp.asarray(mall.reshape(-1, c), MXU_DTYPE), jnp.asarray(masks, F32))


def _hgrn_kernel(aq_ref, af_ref, ai_ref, ag_ref, lbraw_ref, ng_ref, mall_ref, msk_ref,
                 o_ref, st_ref, e_ref, *, layer, n_chunks):
    @pl.when(pl.program_id(1) == 0)
    def _():
        st_ref[...] = jnp.zeros_like(st_ref)

    raw = lbraw_ref[...]
    ex = jnp.exp(raw - jnp.max(raw, axis=0, keepdims=True))
    sm = ex / jnp.sum(ex, axis=0, keepdims=True)
    lb = jnp.zeros((1, A_WIDTH), F32)
    for j in range(1, layer + 1):
        lb = lb + sm[j:j + 1]
    ng = ng_ref[...]
    c = CHUNK

    def chunk_body(ci, carry):
        r0 = pl.multiple_of(ci * c, c)
        rows = pl.ds(r0, c)
        fp = af_ref[rows, :]
        logf = jnp.minimum(fp, 0.0) - jnp.log1p(jnp.exp(-jnp.abs(fp))) + jnp.log1p(lb * jnp.exp(-fp))
        kk = (1.0 - lb) * (1.0 / (1.0 + jnp.exp(fp)))
        qf = _silu(aq_ref[rows, :])
        vv = ai_ref[rows, :].astype(MXU_DTYPE)
        gate = _silu(ag_ref[rows, :])
        h1 = logf.astype(MXU_DTYPE)
        r1 = logf - h1.astype(F32)
        h2 = r1.astype(MXU_DTYPE)
        h3 = (r1 - h2.astype(F32)).astype(MXU_DTYPE)
        mall = mall_ref[...]
        e_ref[...] = jnp.exp(_dot(mall, h1) + _dot(mall, h2) + _dot(mall, h3))
        for h in range(A_HEADS):
            hs = slice(h * A_HEAD_DIM, (h + 1) * A_HEAD_DIM)
            qh, kh, vh = qf[:, hs], kk[:, hs], vv[:, hs]
            st = st_ref[h]
            o = _dot_nt((qh * e_ref[0:c, hs]).astype(MXU_DTYPE), st.astype(MXU_DTYPE))
            s = msk_ref[0] * _dot_nt(qh.astype(MXU_DTYPE), kh.astype(MXU_DTYPE))
            for lvl in range(1, N_LEVELS + 1):
                el = e_ref[lvl * c:(lvl + 1) * c, hs]
                s = s + msk_ref[lvl] * _dot_nt((qh * el).astype(MXU_DTYPE), (kh * el).astype(MXU_DTYPE))
            o = o + _dot(s.astype(MXU_DTYPE), vh)
            e_last = e_ref[(N_LEVELS + 1) * c:(N_LEVELS + 2) * c, hs]
            st_ref[h] = st * e_ref[c - 1:c, hs] + _dot_tn(vh, (kh * e_last).astype(MXU_DTYPE))
            o = o * lax.rsqrt(jnp.mean(o * o, axis=-1, keepdims=True) + RMS_EPS) * ng * gate[:, hs]
            o_ref[rows, hs] = o.astype(o_ref.dtype)
        return carry

    lax.fori_loop(0, n_chunks, chunk_body, 0)


def _hgrn(p1, lbraw, norm_g, layer, bsz, seq, tc):
    m = bsz * seq
    nt = seq // tc
    mall, masks = _hgrn_tables()
    depth = lbraw.shape[0]
    col = lambda cb: pl.BlockSpec((tc, A_WIDTH), lambda b, t, cb=cb: (b * nt + t, cb))
    return pl.pallas_call(
        functools.partial(_hgrn_kernel, layer=layer, n_chunks=tc // CHUNK),
        name="hgrn2",
        grid=(bsz, nt),
        in_specs=[col(0), col(1), col(2), col(3),
                  pl.BlockSpec((depth, A_WIDTH), lambda b, t: (0, 0)),
                  pl.BlockSpec((None, 1, A_HEAD_DIM), lambda b, t: (layer, 0, 0)),
                  pl.BlockSpec(mall.shape, lambda b, t: (0, 0)),
                  pl.BlockSpec(masks.shape, lambda b, t: (0, 0, 0))],
        out_specs=pl.BlockSpec((tc, A_WIDTH), lambda b, t: (b * nt + t, 0)),
        out_shape=jax.ShapeDtypeStruct((m, A_WIDTH), MXU_DTYPE),
        scratch_shapes=[pltpu.VMEM((A_HEADS, A_HEAD_DIM, A_HEAD_DIM), F32),
                        pltpu.VMEM(((N_LEVELS + 2) * CHUNK, A_WIDTH), F32)],
        compiler_params=_params("arbitrary", "arbitrary"),
    )(p1, p1, p1, p1, lbraw, norm_g, mall, masks)


def _pair_partner(x, lo):
    return jnp.where(lo, pltpu.roll(x, LANES - IDX_DIM // 2, 1), pltpu.roll(x, IDX_DIM // 2, 1))


def _prep_kernel(q_ref, k_ref, v_ref, iq_ref, kw_ref, cos_ref, sin_ref, kng_ref, knb_ref,
                 qr_ref, kr_ref, vt_ref, qir_ref, kiab_ref, wt_ref):
    c_h, s_h = cos_ref[:, 0:LANES], sin_ref[:, 0:LANES]
    c_i, s_i = cos_ref[:, LANES:2 * LANES], sin_ref[:, LANES:2 * LANES]
    for h in range(B_HEADS):
        hs = slice(h * B_HEAD_DIM, (h + 1) * B_HEAD_DIM)
        x = q_ref[:, hs]
        qr_ref[:, hs] = ((x * c_h + pltpu.roll(x, B_HEAD_DIM // 2, 1) * s_h) * ATTN_SCALE).astype(qr_ref.dtype)
        x = k_ref[:, hs]
        kr_ref[:, hs] = (x * c_h + pltpu.roll(x, B_HEAD_DIM // 2, 1) * s_h).astype(kr_ref.dtype)
    vt_ref[...] = v_ref[...].T.astype(vt_ref.dtype)
    lane = lax.broadcasted_iota(I32, (1, LANES), 1)
    lo = (lane % IDX_DIM) < (IDX_DIM // 2)
    for j in range(IDX_WIDTH // LANES):
        js = slice(j * LANES, (j + 1) * LANES)
        x = iq_ref[:, js]
        qir_ref[:, js] = (x * c_i + _pair_partner(x, lo) * s_i).astype(qir_ref.dtype)
    t = kw_ref[...]
    in_k = lane < IDX_DIM
    mu = jnp.sum(jnp.where(in_k, t, 0.0), axis=-1, keepdims=True) / IDX_DIM
    xc = jnp.where(in_k, t - mu, 0.0)
    var = jnp.sum(xc * xc, axis=-1, keepdims=True) / IDX_DIM
    kn = xc * lax.rsqrt(var + LN_EPS) * kng_ref[...] + knb_ref[...]
    ki = jnp.where(in_k, kn * c_i + _pair_partner(kn, lo) * s_i, 0.0)
    kiab_ref[:, 0:LANES] = ki.astype(kiab_ref.dtype)
    kiab_ref[:, LANES:2 * LANES] = pltpu.roll(ki, IDX_DIM, 1).astype(kiab_ref.dtype)
    wt_ref[...] = (t * IDX_SCALE).T


def _prep(p2, p3, cos_t, sin_t, kn_g, kn_b, layer, bsz, seq, ts):
    m = bsz * seq
    nt = seq // ts
    col = lambda cb: pl.BlockSpec((ts, B_WIDTH), lambda i, cb=cb: (i, cb))
    row = lambda w: pl.BlockSpec((ts, w), lambda i: (i, 0))
    vec = pl.BlockSpec((None, 1, LANES), lambda i: (layer, 0, 0))
    return pl.pallas_call(
        _prep_kernel,
        name="dsa_prep",
        grid=(m // ts,),
        in_specs=[col(0), col(1), col(2), col(3), row(LANES), row(2 * LANES), row(2 * LANES), vec, vec],
        out_specs=[row(B_WIDTH), row(B_WIDTH),
                   pl.BlockSpec((None, B_WIDTH, ts), lambda i: (i // nt, 0, i % nt)),
                   row(IDX_WIDTH), row(2 * LANES),
                   pl.BlockSpec((None, LANES, ts), lambda i: (i // nt, 0, i % nt))],
        out_shape=[jax.ShapeDtypeStruct((m, B_WIDTH), MXU_DTYPE),
                   jax.ShapeDtypeStruct((m, B_WIDTH), MXU_DTYPE),
                   jax.ShapeDtypeStruct((bsz, B_WIDTH, seq), MXU_DTYPE),
                   jax.ShapeDtypeStruct((m, IDX_WIDTH), MXU_DTYPE),
                   jax.ShapeDtypeStruct((m, 2 * LANES), MXU_DTYPE),
                   jax.ShapeDtypeStruct((bsz, LANES, seq), F32)],
        compiler_params=_params("arbitrary"),
    )(p2, p2, p2, p2, p3, cos_t, sin_t, kn_g, kn_b)


def _indexer_kernel(kiab_ref, qi_ref, wt_ref, o_ref, keys_ref, j_ref, *, tq, topk, seq):
    qb = pl.program_id(1)
    n_kb = qb + 1
    t_idx = qb * tq + lax.broadcasted_iota(I32, (1, tq), 1)
    q_chunk = t_idx // CHUNK

    def key_rows(kb):
        r0 = pl.multiple_of(kb * tq, tq)
        return r0, r0 + lax.broadcasted_iota(I32, (tq, tq), 0)

    def score_body(kb, carry):
        r0, s_idx = key_rows(kb)
        ka = kiab_ref[pl.ds(r0, tq), 0:LANES]
        kb_ = kiab_ref[pl.ds(r0, tq), LANES:2 * LANES]
        acc = jnp.zeros((tq, tq), F32)
        for j in range(IDX_WIDTH // LANES):
            qt = qi_ref[:, j * LANES:(j + 1) * LANES]
            w0 = wt_ref[IDX_DIM + 2 * j:IDX_DIM + 2 * j + 1, :]
            w1 = wt_ref[IDX_DIM + 2 * j + 1:IDX_DIM + 2 * j + 2, :]
            acc = acc + jnp.maximum(_dot_nt(ka, qt), 0.0) * w0 + jnp.maximum(_dot_nt(kb_, qt), 0.0) * w1
        sc = jnp.where((s_idx // CHUNK) <= q_chunk, acc, MASK_VALUE)
        bits = lax.bitcast_convert_type(sc, I32)
        keys_ref[pl.ds(r0, tq), :] = bits ^ ((bits >> 31) & 0x7FFFFFFF)
        return carry

    lax.fori_loop(0, n_kb, score_body, 0)

    def count(pred):
        def body(kb, c8):
            r0, s_idx = key_rows(kb)
            p = pred(keys_ref[pl.ds(r0, tq), :], s_idx)
            return c8 + jnp.sum(jnp.where(p, 1, 0).astype(I32).reshape(tq // 8, 8, tq), axis=0)
        c8 = lax.fori_loop(0, n_kb, body, jnp.zeros((8, tq), I32))
        return jnp.sum(c8, axis=0, keepdims=True)

    def bit_body(i, thr):
        cand = jnp.where(i == 0, 0, thr | lax.shift_left(jnp.int32(1), 31 - i))
        c = count(lambda x, s_idx: x >= cand)
        return jnp.where(c >= topk, cand, thr)

    thr = lax.fori_loop(0, 32, bit_body, jnp.full((1, tq), _INT_MIN, I32))
    c_ge = count(lambda x, s_idx: x >= thr)
    tie = (c_ge > topk) & (thr > KEY_MASK)
    j_ref[...] = jnp.full(j_ref.shape, seq, I32)

    @pl.when(jnp.max(jnp.where(tie, 1, 0)) > 0)
    def _():
        need = topk - count(lambda x, s_idx: x > thr)

        def jbit_body(i, jj):
            cand = jj | lax.shift_left(jnp.int32(1), (seq - 1).bit_length() - 1 - i)
            c = count(lambda x, s_idx: (x == thr) & (s_idx < cand))
            return jnp.where(c < need, cand, jj)

        jj = lax.fori_loop(0, (seq - 1).bit_length(), jbit_body, jnp.zeros((1, tq), I32))
        j_ref[...] = jnp.broadcast_to(jnp.where(tie, jj, seq), j_ref.shape)

    j_lim = j_ref[0:1, :]

    def out_body(kb, carry):
        r0, s_idx = key_rows(kb)
        x = keys_ref[pl.ds(r0, tq), :]
        sel = ((x > thr) | ((x == thr) & (s_idx <= j_lim))) & ((s_idx // CHUNK) <= q_chunk)
        o_ref[pl.ds(r0, tq), :] = jnp.where(sel, 0.0, MASK_VALUE).astype(o_ref.dtype)
        return carry

    lax.fori_loop(0, n_kb, out_body, 0)

    def fill_body(kb, carry):
        r0 = pl.multiple_of(kb * tq, tq)
        o_ref[pl.ds(r0, tq), :] = jnp.full((tq, tq), MASK_VALUE, o_ref.dtype)
        return carry

    lax.fori_loop(n_kb, seq // tq, fill_body, 0)


def _indexer(kiab, qir, wt, bsz, seq, tq, topk):
    nq = seq // tq
    return pl.pallas_call(
        functools.partial(_indexer_kernel, tq=tq, topk=topk, seq=seq),
        name="dsa_indexer",
        grid=(bsz, nq),
        in_specs=[pl.BlockSpec((seq, 2 * LANES), lambda b, q: (b, 0)),
                  pl.BlockSpec((tq, IDX_WIDTH), lambda b, q: (b * nq + q, 0)),
                  pl.BlockSpec((None, LANES, tq), lambda b, q: (b, 0, q))],
        out_specs=pl.BlockSpec((None, seq, tq), lambda b, q: (b, 0, q)),
        out_shape=jax.ShapeDtypeStruct((bsz, seq, seq), MXU_DTYPE),
        scratch_shapes=[pltpu.VMEM((seq, tq), I32), pltpu.VMEM((8, tq), I32)],
        compiler_params=_params("arbitrary", "arbitrary"),
    )(kiab, qir, wt)


def _attn_kernel(q_ref, k_ref, vt_ref, bias_ref, o_ref, *, tq):
    qb = pl.program_id(1)
    n_kb = qb + 1
    for h in range(B_HEADS):
        hs = slice(h * B_HEAD_DIM, (h + 1) * B_HEAD_DIM)
        qh = q_ref[:, hs]

        def body(kb, carry, hs=hs, qh=qh):
            m_i, l_i, acc = carry
            r0 = pl.multiple_of(kb * tq, tq)
            lg = _dot_nt(k_ref[pl.ds(r0, tq), hs], qh) + bias_ref[pl.ds(r0, tq), :].astype(F32)
            m_new = jnp.maximum(m_i, jnp.max(lg, axis=0, keepdims=True))
            alpha = jnp.exp(m_i - m_new)
            p = jnp.exp(lg - m_new)
            l_new = alpha * l_i + jnp.sum(p, axis=0, keepdims=True)
            acc = alpha * acc + _dot(vt_ref[hs, pl.ds(r0, tq)], p.astype(MXU_DTYPE))
            return m_new, l_new, acc

        init = (jnp.full((1, tq), -jnp.inf, F32), jnp.zeros((1, tq), F32),
                jnp.zeros((B_HEAD_DIM, tq), F32))
        _, l_i, acc = lax.fori_loop(0, n_kb, body, init)
        o_ref[:, hs] = (acc / l_i).T.astype(o_ref.dtype)


def _attention(qr, kr, vt, bias, bsz, seq, tq):
    nq = seq // tq
    m = bsz * seq
    return pl.pallas_call(
        functools.partial(_attn_kernel, tq=tq),
        name="dsa_attention",
        grid=(bsz, nq),
        in_specs=[pl.BlockSpec((tq, B_WIDTH), lambda b, q: (b * nq + q, 0)),
                  _resident((seq, B_WIDTH), lambda b, q: (b, 0)),
                  _resident((None, B_WIDTH, seq), lambda b, q: (b, 0, 0)),
                  pl.BlockSpec((None, seq, tq), lambda b, q: (b, 0, q))],
        out_specs=pl.BlockSpec((tq, B_WIDTH), lambda b, q: (b * nq + q, 0)),
        out_shape=jax.ShapeDtypeStruct((m, B_WIDTH), MXU_DTYPE),
        compiler_params=_params("arbitrary", "arbitrary"),
    )(qr, kr, vt, bias)


def _mix_kernel(ya_ref, yb_ref, ga_ref, gb_ref, x_ref, wa_ref, wb_ref, wo_ref, g_ref, b_ref,
                xo_ref, xbo_ref, *, alpha):
    merged = ga_ref[...].astype(F32) * _dot(ya_ref[...], wa_ref[...]) \
        + gb_ref[...].astype(F32) * _dot(yb_ref[...], wb_ref[...])
    z = alpha * x_ref[...] + _dot(merged.astype(MXU_DTYPE), wo_ref[...])
    xn = _layer_norm(z, g_ref[...], b_ref[...])
    xo_ref[...] = xn
    xbo_ref[...] = xn.astype(xbo_ref.dtype)


def _mix(ya, yb, gates, x, wa, wb, wo, g, b, layer, alpha, tm):
    m, d = x.shape
    row = lambda w: pl.BlockSpec((tm, w), lambda i: (i, 0))
    vec = pl.BlockSpec((None, 1, d), lambda i: (layer, 0, 0))
    return pl.pallas_call(
        functools.partial(_mix_kernel, alpha=alpha),
        name="mix_out_ln",
        grid=(m // tm,),
        in_specs=[row(A_WIDTH), row(B_WIDTH),
                  pl.BlockSpec((tm, d), lambda i: (i, 0)), pl.BlockSpec((tm, d), lambda i: (i, 1)),
                  row(d),
                  _resident((None, A_WIDTH, d), lambda i: (layer, 0, 0)),
                  _resident((None, B_WIDTH, d), lambda i: (layer, 0, 0)),
                  _resident((None, d, d), lambda i: (layer, 0, 0)),
                  vec, vec],
        out_specs=[row(d), row(d)],
        out_shape=[jax.ShapeDtypeStruct((m, d), F32), jax.ShapeDtypeStruct((m, d), MXU_DTYPE)],
        compiler_params=_params("arbitrary"),
    )(ya, yb, gates, gates, x, wa, wb, wo, g, b)


def _ffn_kernel(xb_ref, x_ref, wg_ref, wu_ref, wd_ref, g_ref, b_ref, xo_ref, xbo_ref, acc_ref, *, alpha):
    f = pl.program_id(1)
    xb = xb_ref[...]
    hid = _silu(_dot(xb, wg_ref[...])) * _dot(xb, wu_ref[...])
    contrib = _dot(hid.astype(MXU_DTYPE), wd_ref[...])

    @pl.when(f == 0)
    def _():
        acc_ref[...] = contrib

    @pl.when(f > 0)
    def _():
        acc_ref[...] += contrib

    @pl.when(f == pl.num_programs(1) - 1)
    def _():
        xn = _layer_norm(alpha * x_ref[...] + acc_ref[...], g_ref[...], b_ref[...])
        xo_ref[...] = xn
        xbo_ref[...] = xn.astype(xbo_ref.dtype)


def _ffn(xb, x, wg, wu, wd, g, b, layer, alpha, tm, tf):
    m, d = x.shape
    d_ff = wg.shape[-1]
    row = pl.BlockSpec((tm, d), lambda i, f: (i, 0))
    vec = pl.BlockSpec((None, 1, d), lambda i, f: (layer, 0, 0))
    return pl.pallas_call(
        functools.partial(_ffn_kernel, alpha=alpha),
        name="ffn_ln",
        grid=(m // tm, d_ff // tf),
        in_specs=[row, row,
                  pl.BlockSpec((None, d, tf), lambda i, f: (layer, 0, f)),
                  pl.BlockSpec((None, d, tf), lambda i, f: (layer, 0, f)),
                  pl.BlockSpec((None, tf, d), lambda i, f: (layer, f, 0)),
                  vec, vec],
        out_specs=[row, row],
        out_shape=[jax.ShapeDtypeStruct((m, d), F32), jax.ShapeDtypeStruct((m, d), MXU_DTYPE)],
        scratch_shapes=[pltpu.VMEM((tm, d), F32)],
        compiler_params=_params("arbitrary", "arbitrary"),
    )(xb, x, wg, wu, wd, g, b)


def _ple_kernel(xb_ref, x_ref, p_ref, wg_ref, wp_ref, g_ref, b_ref, xo_ref, xbo_ref, *, alpha):
    ple = _sigmoid(_dot(xb_ref[...], wg_ref[...])) * _dot(p_ref[...].astype(MXU_DTYPE), wp_ref[...])
    xn = _layer_norm(alpha * x_ref[...] + ple, g_ref[...], b_ref[...])
    xo_ref[...] = xn
    xbo_ref[...] = xn.astype(xbo_ref.dtype)


def _ple(xb, x, p, wg, wp, g, b, layer, alpha, tm):
    m, d = x.shape
    pd = p.shape[-1]
    row = pl.BlockSpec((tm, d), lambda i: (i, 0))
    vec = pl.BlockSpec((None, 1, d), lambda i: (layer, 0, 0))
    return pl.pallas_call(
        functools.partial(_ple_kernel, alpha=alpha),
        name="ple_ln",
        grid=(m // tm,),
        in_specs=[row, row,
                  pl.BlockSpec((None, tm, pd), lambda i: (layer, i, 0)),
                  _resident((None, d, d), lambda i: (layer, 0, 0)),
                  _resident((None, pd, d), lambda i: (layer, 0, 0)),
                  vec, vec],
        out_specs=[row, row],
        out_shape=[jax.ShapeDtypeStruct((m, d), F32), jax.ShapeDtypeStruct((m, d), MXU_DTYPE)],
        compiler_params=_params("arbitrary"),
    )(xb, x, p, wg, wp, g, b)


def kernel(x, p, positions, w_in, w_branch_a, w_branch_b, w_out, hgrn_lower_bounds, hgrn_norm_g,
           idx_k_norm_g, idx_k_norm_b, ln_mix_g, ln_mix_b, w_ffn_gate, w_ffn_up, w_ffn_down,
           ln_ffn_g, ln_ffn_b, w_ple_gate, w_ple_proj, ln_ple_g, ln_ple_b):
    bsz, seq, d = x.shape
    depth = w_in.shape[0]
    m = bsz * seq
    alpha = (2 * depth) ** 0.25
    topk = min(TOPK_MAX, seq // 4)
    tq = 256
    assert seq % tq == 0 and topk <= tq and tq % CHUNK == 0
    tm = min(512, m)
    ts = min(512, seq)
    gate_col0 = 4 * A_WIDTH + 3 * B_WIDTH + IDX_WIDTH + IDX_DIM + IDX_HEADS
    assert w_in.shape[-1] == gate_col0 + 2 * d

    cast = lambda w: w.astype(MXU_DTYPE)
    w_in_b = cast(w_in)
    w_gates = cast(w_in[:, :, gate_col0:])
    wa, wb, wo = cast(w_branch_a), cast(w_branch_b), cast(w_out)
    wg, wu, wd = cast(w_ffn_gate), cast(w_ffn_up), cast(w_ffn_down)
    wpg, wpp = cast(w_ple_gate), cast(w_ple_proj)
    vec3 = lambda v: v.astype(F32).reshape(depth, 1, -1)
    pad_idx = lambda v: jnp.pad(v.astype(F32), ((0, 0), (0, LANES - IDX_DIM))).reshape(depth, 1, LANES)
    kn_g, kn_b = pad_idx(idx_k_norm_g), pad_idx(idx_k_norm_b)
    norm_g = vec3(hgrn_norm_g)
    lbraw = hgrn_lower_bounds.astype(F32)
    ln = [(vec3(g_), vec3(b_)) for g_, b_ in
          ((ln_mix_g, ln_mix_b), (ln_ffn_g, ln_ffn_b), (ln_ple_g, ln_ple_b))]
    p2d = p.reshape(depth, m, p.shape[-1])

    cos_t, sin_t = _rope_tables(positions, ts)
    xf = x.reshape(m, d).astype(F32)
    xb = xf.astype(MXU_DTYPE)
    for layer in range(depth):
        p1 = _matmul(xb, w_in_b, layer, 0, 4 * A_WIDTH, tm=tm, tn=1024, out_dtype=F32)
        p2 = _matmul(xb, w_in_b, layer, 4 * A_WIDTH // 1024, 3 * B_WIDTH + IDX_WIDTH,
                     tm=tm, tn=1024, out_dtype=F32)
        p3 = _matmul(xb, w_in_b, layer, (4 * A_WIDTH + 3 * B_WIDTH + IDX_WIDTH) // LANES, LANES,
                     tm=tm, tn=LANES, out_dtype=F32)
        gates = _matmul(xb, w_gates, layer, 0, 2 * d, tm=tm, tn=1024, out_dtype=MXU_DTYPE, act="sigmoid")
        ya = _hgrn(p1, lbraw, norm_g, layer, bsz, seq, min(256, seq))
        qr, kr, vt, qir, kiab, wt = _prep(p2, p3, cos_t, sin_t, kn_g, kn_b, layer, bsz, seq, ts)
        bias = _indexer(kiab, qir, wt, bsz, seq, tq, topk)
        yb = _attention(qr, kr, vt, bias, bsz, seq, tq)
        xf, xb = _mix(ya, yb, gates, xf, wa, wb, wo, ln[0][0], ln[0][1], layer, alpha, min(256, m))
        xf, xb = _ffn(xb, xf, wg, wu, wd, ln[1][0], ln[1][1], layer, alpha, tm, 512)
        xf, xb = _ple(xb, xf, p2d, wpg, wpp, ln[2][0], ln[2][1], layer, alpha, min(256, m))
    return xf.reshape(bsz, seq, d).astype(x.dtype)
```

```python
import functools

import numpy as np
import jax
import jax.numpy as jnp
from jax import lax
from jax.experimental import pallas as pl
from jax.experimental.pallas import tpu as pltpu

F32 = jnp.float32
I32 = jnp.int32
MXU_DTYPE = jnp.bfloat16

CHUNK = 64
A_HEADS = 8
A_HEAD_DIM = 128
B_HEADS = 8
B_HEAD_DIM = 128
IDX_HEADS = 16
IDX_DIM = 64
TOPK_MAX = 256
ROPE_THETA = 10000.0
LN_EPS = 1e-5
RMS_EPS = 1e-6
MASK_VALUE = -1e30
ATTN_SCALE = B_HEAD_DIM ** -0.5
Q_SCALE = ATTN_SCALE * 1.4426950408889634
IDX_SCALE = (IDX_HEADS * IDX_DIM) ** -0.5
A_WIDTH = A_HEADS * A_HEAD_DIM
B_WIDTH = B_HEADS * B_HEAD_DIM
IDX_WIDTH = IDX_HEADS * IDX_DIM
LANES = 128
N_LEVELS = 6
VMEM_LIMIT = 52 * 1024 * 1024

_INT_MIN = -2 ** 31


def _float_key(v):
    bits = int(np.float32(v).view(np.int32))
    return bits ^ ((bits >> 31) & 0x7FFFFFFF)


KEY_MASK = _float_key(MASK_VALUE)


def _dot(a, b):
    return jnp.dot(a, b, preferred_element_type=F32)


def _dot_nt(a, b):
    return lax.dot_general(a, b, (((1,), (1,)), ((), ())), preferred_element_type=F32)


def _dot_tn(a, b):
    return lax.dot_general(a, b, (((0,), (0,)), ((), ())), preferred_element_type=F32)


def _sigmoid(x):
    return 1.0 / (1.0 + jnp.exp(-x))


def _silu(x):
    return x * _sigmoid(x)


def _layer_norm(z, g, b):
    mu = jnp.mean(z, axis=-1, keepdims=True)
    zc = z - mu
    var = jnp.mean(zc * zc, axis=-1, keepdims=True)
    return zc * lax.rsqrt(var + LN_EPS) * g + b


def _params(*sem, flags=None):
    return pltpu.CompilerParams(dimension_semantics=sem, vmem_limit_bytes=VMEM_LIMIT, flags=flags)


def _resident(block_shape, index_map):
    return pl.BlockSpec(block_shape, index_map, pipeline_mode=pl.Buffered(1))


def _mm_kernel(x_ref, w_ref, o_ref, *, act):
    acc = _dot(x_ref[...], w_ref[...])
    if act == "sigmoid":
        acc = _sigmoid(acc)
    o_ref[...] = acc.astype(o_ref.dtype)


def _matmul(xb, w, layer, col_blk0, ncols, *, tm, tn, out_dtype, act=None):
    m, k = xb.shape
    return pl.pallas_call(
        functools.partial(_mm_kernel, act=act),
        name="proj_" + (act or "linear"),
        grid=(ncols // tn, m // tm),
        in_specs=[pl.BlockSpec((tm, k), lambda j, i: (i, 0)),
                  pl.BlockSpec((None, k, tn), lambda j, i: (layer, 0, col_blk0 + j))],
        out_specs=pl.BlockSpec((tm, tn), lambda j, i: (i, j)),
        out_shape=jax.ShapeDtypeStruct((m, ncols), out_dtype),
        compiler_params=_params("arbitrary", "arbitrary"),
    )(xb, w)


def _rope_kernel(pos_ref, freq_ref, sign_ref, cos_ref, sin_ref):
    ang = pos_ref[...].astype(F32) * freq_ref[...]
    cos_ref[...] = jnp.cos(ang)
    sin_ref[...] = jnp.sin(ang) * sign_ref[...]


def _rope_tables(positions, ts):
    m = positions.size
    half_h, half_i = B_HEAD_DIM // 2, IDX_DIM // 2
    lane = np.arange(LANES)
    inv_h = ROPE_THETA ** (-jnp.arange(half_h, dtype=F32) / half_h)
    inv_i = ROPE_THETA ** (-jnp.arange(half_i, dtype=F32) / half_i)
    freq = jnp.concatenate([jnp.tile(inv_h, LANES // half_h), jnp.tile(inv_i, LANES // half_i)])[None, :]
    s_h = np.where(lane % B_HEAD_DIM < half_h, -1.0, 1.0)
    s_i = np.where(lane % IDX_DIM < half_i, -1.0, 1.0)
    sign = jnp.asarray(np.concatenate([s_h, s_i])[None, :], F32)
    return pl.pallas_call(
        _rope_kernel,
        name="rope_tables",
        grid=(m // ts,),
        in_specs=[pl.BlockSpec((ts, 1), lambda i: (i, 0)),
                  pl.BlockSpec((1, 2 * LANES), lambda i: (0, 0)),
                  pl.BlockSpec((1, 2 * LANES), lambda i: (0, 0))],
        out_specs=[pl.BlockSpec((ts, 2 * LANES), lambda i: (i, 0))] * 2,
        out_shape=[jax.ShapeDtypeStruct((m, 2 * LANES), F32)] * 2,
        compiler_params=_params("arbitrary"),
    )(positions.reshape(m, 1), freq, sign)


def _hgrn_tables():
    c = CHUNK
    mall = np.zeros((N_LEVELS + 2, c, c), np.float32)
    masks = np.zeros((N_LEVELS + 1, c, c), np.float32)
    t = np.arange(c)
    mall[0] = (t[None, :] <= t[:, None])
    masks[0] = np.eye(c)
    for lvl in range(1, N_LEVELS + 1):
        half = c >> lvl
        mid = (t // (2 * half)) * (2 * half) + half - 1
        right = t > mid
        j = t[None, :]
        mall[lvl] = np.where(right[:, None], (j > mid[:, None]) & (j <= t[:, None]),
                             (j > t[:, None]) & (j <= mid[:, None]))
        same = (t[:, None] // (2 * half)) == (t[None, :] // (2 * half))
        masks[lvl] = same & right[:, None] & (~right[None, :])
    mall[N_LEVELS + 1] = (t[None, :] > t[:, None])
    return (jnp.asarray(mall.reshape(-1, c), MXU_DTYPE), jnp.asarray(masks, F32))


def _hgrn_kernel(aq_ref, af_ref, ai_ref, ag_ref, lbraw_ref, ng_ref, mall_ref, msk_ref,
                 o_ref, st_ref, e_ref, s_ref, oi_ref, *, layer, n_chunks):
    @pl.when(pl.program_id(1) == 0)
    def _():
        st_ref[...] = jnp.zeros_like(st_ref)

    raw = lbraw_ref[...]
    ex = jnp.exp(raw - jnp.max(raw, axis=0, keepdims=True))
    sm = ex / jnp.sum(ex, axis=0, keepdims=True)
    lb = jnp.zeros((1, A_WIDTH), F32)
    for j in range(1, layer + 1):
        lb = lb + sm[j:j + 1]
    ng = ng_ref[...]
    c = CHUNK

    def chunk_body(ci, carry):
        r0 = pl.multiple_of(ci * c, c)
        rows = pl.ds(r0, c)
        fp = af_ref[rows, :]
        logf = jnp.minimum(fp, 0.0) - jnp.log1p(jnp.exp(-jnp.abs(fp))) + jnp.log1p(lb * jnp.exp(-fp))
        kk = (1.0 - lb) * (1.0 / (1.0 + jnp.exp(fp)))
        qf = _silu(aq_ref[rows, :])
        vv = ai_ref[rows, :].astype(MXU_DTYPE)
        gate = _silu(ag_ref[rows, :])
        h1 = logf.astype(MXU_DTYPE)
        r1 = logf - h1.astype(F32)
        h2 = r1.astype(MXU_DTYPE)
        h3 = (r1 - h2.astype(F32)).astype(MXU_DTYPE)
        mall = mall_ref[...]
        e_ref[...] = jnp.exp(_dot(mall, h1) + _dot(mall, h2) + _dot(mall, h3))
        for h in range(A_HEADS):
            hs = slice(h * A_HEAD_DIM, (h + 1) * A_HEAD_DIM)
            qh, kh, vh = qf[:, hs], kk[:, hs], vv[:, hs]
            st = st_ref[h]
            oi_ref[:, hs] = _dot_nt((qh * e_ref[0:c, hs]).astype(MXU_DTYPE), st.astype(MXU_DTYPE))
            s = msk_ref[0] * _dot_nt(qh.astype(MXU_DTYPE), kh.astype(MXU_DTYPE))
            for lvl in range(1, N_LEVELS + 1):
                el = e_ref[lvl * c:(lvl + 1) * c, hs]
                s = s + msk_ref[lvl] * _dot_nt((qh * el).astype(MXU_DTYPE), (kh * el).astype(MXU_DTYPE))
            s_ref[h] = s.astype(s_ref.dtype)
            e_last = e_ref[(N_LEVELS + 1) * c:(N_LEVELS + 2) * c, hs]
            st_ref[h] = st * e_ref[c - 1:c, hs] + _dot_tn(vh, (kh * e_last).astype(MXU_DTYPE))
        for h in range(A_HEADS):
            hs = slice(h * A_HEAD_DIM, (h + 1) * A_HEAD_DIM)
            o = oi_ref[:, hs] + _dot(s_ref[h], vv[:, hs])
            o = o * lax.rsqrt(jnp.mean(o * o, axis=-1, keepdims=True) + RMS_EPS) * ng * gate[:, hs]
            o_ref[rows, hs] = o.astype(o_ref.dtype)
        return carry

    lax.fori_loop(0, n_chunks, chunk_body, 0)


def _hgrn(p1, lbraw, norm_g, layer, bsz, seq, tc):
    m = bsz * seq
    nt = seq // tc
    mall, masks = _hgrn_tables()
    depth = lbraw.shape[0]
    col = lambda cb: pl.BlockSpec((tc, A_WIDTH), lambda b, t, cb=cb: (b * nt + t, cb))
    return pl.pallas_call(
        functools.partial(_hgrn_kernel, layer=layer, n_chunks=tc // CHUNK),
        name="hgrn2",
        grid=(bsz, nt),
        in_specs=[col(0), col(1), col(2), col(3),
                  pl.BlockSpec((depth, A_WIDTH), lambda b, t: (0, 0)),
                  pl.BlockSpec((None, 1, A_HEAD_DIM), lambda b, t: (layer, 0, 0)),
                  pl.BlockSpec(mall.shape, lambda b, t: (0, 0)),
                  pl.BlockSpec(masks.shape, lambda b, t: (0, 0, 0))],
        out_specs=pl.BlockSpec((tc, A_WIDTH), lambda b, t: (b * nt + t, 0)),
        out_shape=jax.ShapeDtypeStruct((m, A_WIDTH), MXU_DTYPE),
        scratch_shapes=[pltpu.VMEM((A_HEADS, A_HEAD_DIM, A_HEAD_DIM), F32),
                        pltpu.VMEM(((N_LEVELS + 2) * CHUNK, A_WIDTH), F32),
                        pltpu.VMEM((A_HEADS, CHUNK, CHUNK), MXU_DTYPE),
                        pltpu.VMEM((CHUNK, A_WIDTH), F32)],
        compiler_params=_params("arbitrary", "arbitrary"),
    )(p1, p1, p1, p1, lbraw, norm_g, mall, masks)


def _pair_partner(x, lo):
    return jnp.where(lo, pltpu.roll(x, LANES - IDX_DIM // 2, 1), pltpu.roll(x, IDX_DIM // 2, 1))


def _prep_kernel(q_ref, k_ref, v_ref, iq_ref, kw_ref, cos_ref, sin_ref, kng_ref, knb_ref,
                 qr_ref, kr_ref, vt_ref, qir_ref, kiab_ref, wt_ref):
    c_h, s_h = cos_ref[:, 0:LANES], sin_ref[:, 0:LANES]
    c_i, s_i = cos_ref[:, LANES:2 * LANES], sin_ref[:, LANES:2 * LANES]
    for h in range(B_HEADS):
        hs = slice(h * B_HEAD_DIM, (h + 1) * B_HEAD_DIM)
        x = q_ref[:, hs]
        qr_ref[:, hs] = ((x * c_h + pltpu.roll(x, B_HEAD_DIM // 2, 1) * s_h) * Q_SCALE).astype(qr_ref.dtype)
        x = k_ref[:, hs]
        kr_ref[:, hs] = (x * c_h + pltpu.roll(x, B_HEAD_DIM // 2, 1) * s_h).astype(kr_ref.dtype)
    vt_ref[...] = v_ref[...].T.astype(vt_ref.dtype)
    lane = lax.broadcasted_iota(I32, (1, LANES), 1)
    lo = (lane % IDX_DIM) < (IDX_DIM // 2)
    for j in range(IDX_WIDTH // LANES):
        js = slice(j * LANES, (j + 1) * LANES)
        x = iq_ref[:, js]
        qir_ref[:, js] = (x * c_i + _pair_partner(x, lo) * s_i).astype(qir_ref.dtype)
    t = kw_ref[...]
    in_k = lane < IDX_DIM
    mu = jnp.sum(jnp.where(in_k, t, 0.0), axis=-1, keepdims=True) / IDX_DIM
    xc = jnp.where(in_k, t - mu, 0.0)
    var = jnp.sum(xc * xc, axis=-1, keepdims=True) / IDX_DIM
    kn = xc * lax.rsqrt(var + LN_EPS) * kng_ref[...] + knb_ref[...]
    ki = jnp.where(in_k, kn * c_i + _pair_partner(kn, lo) * s_i, 0.0)
    kiab_ref[:, 0:LANES] = ki.astype(kiab_ref.dtype)
    kiab_ref[:, LANES:2 * LANES] = pltpu.roll(ki, IDX_DIM, 1).astype(kiab_ref.dtype)
    wt_ref[...] = (t * IDX_SCALE).T


def _prep(p2, p3, cos_t, sin_t, kn_g, kn_b, layer, bsz, seq, ts):
    m = bsz * seq
    nt = seq // ts
    col = lambda cb: pl.BlockSpec((ts, B_WIDTH), lambda i, cb=cb: (i, cb))
    row = lambda w: pl.BlockSpec((ts, w), lambda i: (i, 0))
    vec = pl.BlockSpec((None, 1, LANES), lambda i: (layer, 0, 0))
    return pl.pallas_call(
        _prep_kernel,
        name="dsa_prep",
        grid=(m // ts,),
        in_specs=[col(0), col(1), col(2), col(3), row(LANES), row(2 * LANES), row(2 * LANES), vec, vec],
        out_specs=[row(B_WIDTH), row(B_WIDTH),
                   pl.BlockSpec((None, B_WIDTH, ts), lambda i: (i // nt, 0, i % nt)),
                   row(IDX_WIDTH), row(2 * LANES),
                   pl.BlockSpec((None, LANES, ts), lambda i: (i // nt, 0, i % nt))],
        out_shape=[jax.ShapeDtypeStruct((m, B_WIDTH), MXU_DTYPE),
                   jax.ShapeDtypeStruct((m, B_WIDTH), MXU_DTYPE),
                   jax.ShapeDtypeStruct((bsz, B_WIDTH, seq), MXU_DTYPE),
                   jax.ShapeDtypeStruct((m, IDX_WIDTH), MXU_DTYPE),
                   jax.ShapeDtypeStruct((m, 2 * LANES), MXU_DTYPE),
                   jax.ShapeDtypeStruct((bsz, LANES, seq), F32)],
        compiler_params=_params("arbitrary"),
    )(p2, p2, p2, p2, p3, cos_t, sin_t, kn_g, kn_b)


def _indexer_kernel(kiab_ref, qi_ref, wt_ref, o_ref, keys_ref, *, tq, topk, seq):
    qb = pl.program_id(1)
    n_kb = qb + 1
    t_idx = qb * tq + lax.broadcasted_iota(I32, (1, tq), 1)
    q_chunk = t_idx // CHUNK

    def key_rows(kb):
        r0 = pl.multiple_of(kb * tq, tq)
        return r0, r0 + lax.broadcasted_iota(I32, (tq, tq), 0)

    def score_body(kb, carry):
        r0, s_idx = key_rows(kb)
        ka = kiab_ref[pl.ds(r0, tq), 0:LANES]
        kb_ = kiab_ref[pl.ds(r0, tq), LANES:2 * LANES]
        acc = jnp.zeros((tq, tq), F32)
        for j in range(IDX_WIDTH // LANES):
            qt = qi_ref[:, j * LANES:(j + 1) * LANES]
            w0 = wt_ref[IDX_DIM + 2 * j:IDX_DIM + 2 * j + 1, :]
            w1 = wt_ref[IDX_DIM + 2 * j + 1:IDX_DIM + 2 * j + 2, :]
            acc = acc + jnp.maximum(_dot_nt(ka, qt), 0.0) * w0 + jnp.maximum(_dot_nt(kb_, qt), 0.0) * w1
        sc = jnp.where((s_idx // CHUNK) <= q_chunk, acc, MASK_VALUE)
        bits = lax.bitcast_convert_type(sc, I32)
        keys_ref[pl.ds(r0, tq), :] = bits ^ ((bits >> 31) & 0x7FFFFFFF)
        return carry

    lax.fori_loop(0, n_kb, score_body, 0)

    def count(pred):
        def body(kb, c8):
            r0, s_idx = key_rows(kb)
            p = pred(keys_ref[pl.ds(r0, tq), :], s_idx)
            return c8 + jnp.sum(jnp.where(p, 1, 0).astype(I32).reshape(tq // 8, 8, tq), axis=0)
        c8 = lax.fori_loop(0, n_kb, body, jnp.zeros((8, tq), I32))
        return jnp.sum(c8, axis=0, keepdims=True)

    def bit_body(i, thr):
        cand = jnp.where(i == 0, 0, thr | lax.shift_left(jnp.int32(1), 31 - i))
        c = count(lambda x, s_idx: x >= cand)
        return jnp.where(c >= topk, cand, thr)

    thr = lax.fori_loop(0, 32, bit_body, jnp.full((1, tq), _INT_MIN, I32))
    c_ge = count(lambda x, s_idx: x >= thr)
    tie = (c_ge > topk) & (thr > KEY_MASK)
    any_tie = jnp.max(jnp.where(tie, 1, 0)) > 0

    @pl.when(any_tie)
    def _():
        need = topk - count(lambda x, s_idx: x > thr)

        def jbit_body(i, jj):
            cand = jj | lax.shift_left(jnp.int32(1), (seq - 1).bit_length() - 1 - i)
            c = count(lambda x, s_idx: (x == thr) & (s_idx < cand))
            return jnp.where(c < need, cand, jj)

        jj = lax.fori_loop(0, (seq - 1).bit_length(), jbit_body, jnp.zeros((1, tq), I32))
        j_lim = jnp.where(tie, jj, seq)

        def out_body(kb, carry):
            r0, s_idx = key_rows(kb)
            x = keys_ref[pl.ds(r0, tq), :]
            sel = ((x > thr) | ((x == thr) & (s_idx <= j_lim))) & ((s_idx // CHUNK) <= q_chunk)
            o_ref[pl.ds(r0, tq), :] = jnp.where(sel, 0.0, MASK_VALUE).astype(o_ref.dtype)
            return carry

        lax.fori_loop(0, n_kb, out_body, 0)

    @pl.when(jnp.logical_not(any_tie))
    def _():
        def out_body(kb, carry):
            r0 = pl.multiple_of(kb * tq, tq)
            x = keys_ref[pl.ds(r0, tq), :]
            o_ref[pl.ds(r0, tq), :] = jnp.where(x >= thr, 0.0, MASK_VALUE).astype(o_ref.dtype)
            return carry

        lax.fori_loop(0, qb, out_body, 0)
        r0, s_idx = key_rows(qb)
        x = keys_ref[pl.ds(r0, tq), :]
        sel = jnp.where((s_idx // CHUNK) <= q_chunk, jnp.where(x >= thr, 0.0, MASK_VALUE), MASK_VALUE)
        o_ref[pl.ds(r0, tq), :] = sel.astype(o_ref.dtype)

    def fill_body(kb, carry):
        r0 = pl.multiple_of(kb * tq, tq)
        o_ref[pl.ds(r0, tq), :] = jnp.full((tq, tq), MASK_VALUE, o_ref.dtype)
        return carry

    lax.fori_loop(n_kb, seq // tq, fill_body, 0)


def _indexer(kiab, qir, wt, bsz, seq, tq, topk):
    nq = seq // tq
    return pl.pallas_call(
        functools.partial(_indexer_kernel, tq=tq, topk=topk, seq=seq),
        name="dsa_indexer",
        grid=(bsz, nq),
        in_specs=[pl.BlockSpec((seq, 2 * LANES), lambda b, q: (b, 0)),
                  pl.BlockSpec((tq, IDX_WIDTH), lambda b, q: (b * nq + q, 0)),
                  pl.BlockSpec((None, LANES, tq), lambda b, q: (b, 0, q))],
        out_specs=pl.BlockSpec((None, seq, tq), lambda b, q: (b, 0, q)),
        out_shape=jax.ShapeDtypeStruct((bsz, seq, seq), MXU_DTYPE),
        scratch_shapes=[pltpu.VMEM((seq, tq), I32)],
        compiler_params=_params("arbitrary", "arbitrary"),
    )(kiab, qir, wt)


def _attn_kernel(q_ref, k_ref, vt_ref, bias_ref, o_ref, m_ref, l_ref, acc_ref, lg_ref, *, tq):
    qb = pl.program_id(1)
    n_kb = qb + 1
    m_ref[...] = jnp.full(m_ref.shape, -jnp.inf, F32)
    l_ref[...] = jnp.zeros(l_ref.shape, F32)
    acc_ref[...] = jnp.zeros(acc_ref.shape, F32)

    def body(kb, carry):
        r0 = pl.multiple_of(kb * tq, tq)
        bias = bias_ref[pl.ds(r0, tq), :].astype(F32)
        for h in range(B_HEADS):
            hs = slice(h * B_HEAD_DIM, (h + 1) * B_HEAD_DIM)
            lg_ref[h] = _dot_nt(k_ref[pl.ds(r0, tq), hs], q_ref[:, hs]) + bias
        for h in range(B_HEADS):
            hs = slice(h * B_HEAD_DIM, (h + 1) * B_HEAD_DIM)
            lg = lg_ref[h]
            m_i = m_ref[h]
            m_new = jnp.maximum(m_i, jnp.max(lg, axis=0, keepdims=True))
            alpha = jnp.exp2(m_i - m_new)
            p = jnp.exp2(lg - m_new)
            l_ref[h] = alpha * l_ref[h] + jnp.sum(p, axis=0, keepdims=True)
            m_ref[h] = m_new
            acc_ref[h] = alpha * acc_ref[h] + _dot(vt_ref[hs, pl.ds(r0, tq)], p.astype(MXU_DTYPE))
        return carry

    lax.fori_loop(0, n_kb, body, 0)
    for h in range(B_HEADS):
        hs = slice(h * B_HEAD_DIM, (h + 1) * B_HEAD_DIM)
        o_ref[:, hs] = (acc_ref[h] / l_ref[h]).T.astype(o_ref.dtype)


def _attention(qr, kr, vt, bias, bsz, seq, tq):
    nq = seq // tq
    m = bsz * seq
    return pl.pallas_call(
        functools.partial(_attn_kernel, tq=tq),
        name="dsa_attention",
        grid=(bsz, nq),
        in_specs=[pl.BlockSpec((tq, B_WIDTH), lambda b, q: (b * nq + q, 0)),
                  _resident((seq, B_WIDTH), lambda b, q: (b, 0)),
                  _resident((None, B_WIDTH, seq), lambda b, q: (b, 0, 0)),
                  pl.BlockSpec((None, seq, tq), lambda b, q: (b, 0, q))],
        out_specs=pl.BlockSpec((tq, B_WIDTH), lambda b, q: (b * nq + q, 0)),
        out_shape=jax.ShapeDtypeStruct((m, B_WIDTH), MXU_DTYPE),
        scratch_shapes=[pltpu.VMEM((B_HEADS, 1, tq), F32), pltpu.VMEM((B_HEADS, 1, tq), F32),
                        pltpu.VMEM((B_HEADS, B_HEAD_DIM, tq), F32),
                        pltpu.VMEM((B_HEADS, tq, tq), F32)],
        compiler_params=_params("arbitrary", "arbitrary"),
    )(qr, kr, vt, bias)


def _mix_kernel(ya_ref, yb_ref, ga_ref, gb_ref, x_ref, wa_ref, wb_ref, wo_ref, g_ref, b_ref,
                xo_ref, xbo_ref, *, alpha):
    merged = ga_ref[...].astype(F32) * _dot(ya_ref[...], wa_ref[...]) \
        + gb_ref[...].astype(F32) * _dot(yb_ref[...], wb_ref[...])
    z = alpha * x_ref[...] + _dot(merged.astype(MXU_DTYPE), wo_ref[...])
    xn = _layer_norm(z, g_ref[...], b_ref[...])
    xo_ref[...] = xn
    xbo_ref[...] = xn.astype(xbo_ref.dtype)


def _mix(ya, yb, gates, x, wa, wb, wo, g, b, layer, alpha, tm):
    m, d = x.shape
    row = lambda w: pl.BlockSpec((tm, w), lambda i: (i, 0))
    vec = pl.BlockSpec((None, 1, d), lambda i: (layer, 0, 0))
    return pl.pallas_call(
        functools.partial(_mix_kernel, alpha=alpha),
        name="mix_out_ln",
        grid=(m // tm,),
        in_specs=[row(A_WIDTH), row(B_WIDTH),
                  pl.BlockSpec((tm, d), lambda i: (i, 0)), pl.BlockSpec((tm, d), lambda i: (i, 1)),
                  row(d),
                  _resident((None, A_WIDTH, d), lambda i: (layer, 0, 0)),
                  _resident((None, B_WIDTH, d), lambda i: (layer, 0, 0)),
                  _resident((None, d, d), lambda i: (layer, 0, 0)),
                  vec, vec],
        out_specs=[row(d), row(d)],
        out_shape=[jax.ShapeDtypeStruct((m, d), F32), jax.ShapeDtypeStruct((m, d), MXU_DTYPE)],
        compiler_params=_params("arbitrary"),
    )(ya, yb, gates, gates, x, wa, wb, wo, g, b)


def _ffn_kernel(xb_ref, x_ref, wg_ref, wu_ref, wd_ref, g_ref, b_ref, xo_ref, xbo_ref, acc_ref, *, alpha):
    f = pl.program_id(1)
    xb = xb_ref[...]
    hid = _silu(_dot(xb, wg_ref[...])) * _dot(xb, wu_ref[...])
    contrib = _dot(hid.astype(MXU_DTYPE), wd_ref[...])

    @pl.when(f == 0)
    def _():
        acc_ref[...] = contrib

    @pl.when(f > 0)
    def _():
        acc_ref[...] += contrib

    @pl.when(f == pl.num_programs(1) - 1)
    def _():
        xn = _layer_norm(alpha * x_ref[...] + acc_ref[...], g_ref[...], b_ref[...])
        xo_ref[...] = xn
        xbo_ref[...] = xn.astype(xbo_ref.dtype)


def _ffn(xb, x, wg, wu, wd, g, b, layer, alpha, tm, tf):
    m, d = x.shape
    d_ff = wg.shape[-1]
    row = pl.BlockSpec((tm, d), lambda i, f: (i, 0))
    vec = pl.BlockSpec((None, 1, d), lambda i, f: (layer, 0, 0))
    return pl.pallas_call(
        functools.partial(_ffn_kernel, alpha=alpha),
        name="ffn_ln",
        grid=(m // tm, d_ff // tf),
        in_specs=[row, row,
                  pl.BlockSpec((None, d, tf), lambda i, f: (layer, 0, f)),
                  pl.BlockSpec((None, d, tf), lambda i, f: (layer, 0, f)),
                  pl.BlockSpec((None, tf, d), lambda i, f: (layer, f, 0)),
                  vec, vec],
        out_specs=[row, row],
        out_shape=[jax.ShapeDtypeStruct((m, d), F32), jax.ShapeDtypeStruct((m, d), MXU_DTYPE)],
        scratch_shapes=[pltpu.VMEM((tm, d), F32)],
        compiler_params=_params("arbitrary", "arbitrary"),
    )(xb, x, wg, wu, wd, g, b)


def _ple_kernel(xb_ref, x_ref, p_ref, wg_ref, wp_ref, g_ref, b_ref, xo_ref, xbo_ref, *, alpha):
    ple = _sigmoid(_dot(xb_ref[...], wg_ref[...])) * _dot(p_ref[...].astype(MXU_DTYPE), wp_ref[...])
    xn = _layer_norm(alpha * x_ref[...] + ple, g_ref[...], b_ref[...])
    xo_ref[...] = xn
    xbo_ref[...] = xn.astype(xbo_ref.dtype)


def _ple(xb, x, p, wg, wp, g, b, layer, alpha, tm):
    m, d = x.shape
    pd = p.shape[-1]
    row = pl.BlockSpec((tm, d), lambda i: (i, 0))
    vec = pl.BlockSpec((None, 1, d), lambda i: (layer, 0, 0))
    return pl.pallas_call(
        functools.partial(_ple_kernel, alpha=alpha),
        name="ple_ln",
        grid=(m // tm,),
        in_specs=[row, row,
                  pl.BlockSpec((None, tm, pd), lambda i: (layer, i, 0)),
                  _resident((None, d, d), lambda i: (layer, 0, 0)),
                  _resident((None, pd, d), lambda i: (layer, 0, 0)),
                  vec, vec],
        out_specs=[row, row],
        out_shape=[jax.ShapeDtypeStruct((m, d), F32), jax.ShapeDtypeStruct((m, d), MXU_DTYPE)],
        compiler_params=_params("arbitrary"),
    )(xb, x, p, wg, wp, g, b)


def kernel(x, p, positions, w_in, w_branch_a, w_branch_b, w_out, hgrn_lower_bounds, hgrn_norm_g,
           idx_k_norm_g, idx_k_norm_b, ln_mix_g, ln_mix_b, w_ffn_gate, w_ffn_up, w_ffn_down,
           ln_ffn_g, ln_ffn_b, w_ple_gate, w_ple_proj, ln_ple_g, ln_ple_b):
    bsz, seq, d = x.shape
    depth = w_in.shape[0]
    m = bsz * seq
    alpha = (2 * depth) ** 0.25
    topk = min(TOPK_MAX, seq // 4)
    tq = 256
    assert seq % tq == 0 and topk <= tq and tq % CHUNK == 0
    tm = min(512, m)
    ts = min(512, seq)
    gate_col0 = 4 * A_WIDTH + 3 * B_WIDTH + IDX_WIDTH + IDX_DIM + IDX_HEADS
    assert w_in.shape[-1] == gate_col0 + 2 * d

    cast = lambda w: w.astype(MXU_DTYPE)
    w_in_b = cast(w_in)
    w_gates = cast(w_in[:, :, gate_col0:])
    wa, wb, wo = cast(w_branch_a), cast(w_branch_b), cast(w_out)
    wg, wu, wd = cast(w_ffn_gate), cast(w_ffn_up), cast(w_ffn_down)
    wpg, wpp = cast(w_ple_gate), cast(w_ple_proj)
    vec3 = lambda v: v.astype(F32).reshape(depth, 1, -1)
    pad_idx = lambda v: jnp.pad(v.astype(F32), ((0, 0), (0, LANES - IDX_DIM))).reshape(depth, 1, LANES)
    kn_g, kn_b = pad_idx(idx_k_norm_g), pad_idx(idx_k_norm_b)
    norm_g = vec3(hgrn_norm_g)
    lbraw = hgrn_lower_bounds.astype(F32)
    ln = [(vec3(g_), vec3(b_)) for g_, b_ in
          ((ln_mix_g, ln_mix_b), (ln_ffn_g, ln_ffn_b), (ln_ple_g, ln_ple_b))]
    p2d = p.reshape(depth, m, p.shape[-1])

    cos_t, sin_t = _rope_tables(positions, ts)
    xf = x.reshape(m, d).astype(F32)
    xb = xf.astype(MXU_DTYPE)
    for layer in range(depth):
        p1 = _matmul(xb, w_in_b, layer, 0, 4 * A_WIDTH, tm=tm, tn=1024, out_dtype=F32)
        p2 = _matmul(xb, w_in_b, layer, 4 * A_WIDTH // 1024, 3 * B_WIDTH + IDX_WIDTH,
                     tm=tm, tn=1024, out_dtype=F32)
        p3 = _matmul(xb, w_in_b, layer, (4 * A_WIDTH + 3 * B_WIDTH + IDX_WIDTH) // LANES, LANES,
                     tm=tm, tn=LANES, out_dtype=F32)
        gates = _matmul(xb, w_gates, layer, 0, 2 * d, tm=tm, tn=1024, out_dtype=MXU_DTYPE, act="sigmoid")
        ya = _hgrn(p1, lbraw, norm_g, layer, bsz, seq, min(256, seq))
        qr, kr, vt, qir, kiab, wt = _prep(p2, p3, cos_t, sin_t, kn_g, kn_b, layer, bsz, seq, ts)
        bias = _indexer(kiab, qir, wt, bsz, seq, tq, topk)
        yb = _attention(qr, kr, vt, bias, bsz, seq, tq)
        xf, xb = _mix(ya, yb, gates, xf, wa, wb, wo, ln[0][0], ln[0][1], layer, alpha, min(256, m))
        xf, xb = _ffn(xb, xf, wg, wu, wd, ln[1][0], ln[1][1], layer, alpha, tm, 512)
        xf, xb = _ple(xb, xf, p2d, wpg, wpp, ln[2][0], ln[2][1], layer, alpha, min(256, m))
    return xf.reshape(bsz, seq, d).astype(x.dtype)
```

```python
import functools

import numpy as np
import jax
import jax.numpy as jnp
from jax import lax
from jax.experimental import pallas as pl
from jax.experimental.pallas import tpu as pltpu

F32 = jnp.float32
I32 = jnp.int32
I16 = jnp.int16
MXU_DTYPE = jnp.bfloat16

CHUNK = 64
A_HEADS = 8
A_HEAD_DIM = 128
B_HEADS = 8
B_HEAD_DIM = 128
IDX_HEADS = 16
IDX_DIM = 64
TOPK_MAX = 256
ROPE_THETA = 10000.0
LN_EPS = 1e-5
RMS_EPS = 1e-6
MASK_VALUE = -1e30
ATTN_SCALE = B_HEAD_DIM ** -0.5
Q_SCALE = ATTN_SCALE * 1.4426950408889634
IDX_SCALE = (IDX_HEADS * IDX_DIM) ** -0.5
A_WIDTH = A_HEADS * A_HEAD_DIM
B_WIDTH = B_HEADS * B_HEAD_DIM
IDX_WIDTH = IDX_HEADS * IDX_DIM
LANES = 128
N_LEVELS = 6
VMEM_LIMIT = 52 * 1024 * 1024

_INT_MIN = -2 ** 31


def _float_key(v):
    bits = int(np.float32(v).view(np.int32))
    return bits ^ ((bits >> 31) & 0x7FFFFFFF)


KEY_MASK = _float_key(MASK_VALUE)


def _dot(a, b):
    return jnp.dot(a, b, preferred_element_type=F32)


def _dot_nt(a, b):
    return lax.dot_general(a, b, (((1,), (1,)), ((), ())), preferred_element_type=F32)


def _dot_tn(a, b):
    return lax.dot_general(a, b, (((0,), (0,)), ((), ())), preferred_element_type=F32)


def _sigmoid(x):
    return 1.0 / (1.0 + jnp.exp(-x))


def _silu(x):
    return x * _sigmoid(x)


def _layer_norm(z, g, b):
    mu = jnp.mean(z, axis=-1, keepdims=True)
    zc = z - mu
    var = jnp.mean(zc * zc, axis=-1, keepdims=True)
    return zc * lax.rsqrt(var + LN_EPS) * g + b


def _params(*sem, flags=None):
    return pltpu.CompilerParams(dimension_semantics=sem, vmem_limit_bytes=VMEM_LIMIT, flags=flags)


def _resident(block_shape, index_map):
    return pl.BlockSpec(block_shape, index_map, pipeline_mode=pl.Buffered(1))


def _mm_kernel(x_ref, w_ref, o_ref, wb_ref, *, act):
    @pl.when(pl.program_id(1) == 0)
    def _():
        wb_ref[...] = w_ref[...].astype(wb_ref.dtype)

    acc = _dot(x_ref[...], wb_ref[...])
    if act == "sigmoid":
        acc = _sigmoid(acc)
    o_ref[...] = acc.astype(o_ref.dtype)


def _matmul(xb, w, layer, col_blk0, ncols, *, tm, tn, out_dtype, act=None):
    m, k = xb.shape
    return pl.pallas_call(
        functools.partial(_mm_kernel, act=act),
        name="proj_" + (act or "linear"),
        grid=(ncols // tn, m // tm),
        in_specs=[pl.BlockSpec((tm, k), lambda j, i: (i, 0)),
                  pl.BlockSpec((None, k, tn), lambda j, i: (layer, 0, col_blk0 + j))],
        out_specs=pl.BlockSpec((tm, tn), lambda j, i: (i, j)),
        out_shape=jax.ShapeDtypeStruct((m, ncols), out_dtype),
        scratch_shapes=[pltpu.VMEM((k, tn), MXU_DTYPE)],
        compiler_params=_params("arbitrary", "arbitrary"),
    )(xb, w)


def _swiglu_kernel(x_ref, wg_ref, wu_ref, o_ref, wgb_ref, wub_ref):
    @pl.when(pl.program_id(1) == 0)
    def _():
        wgb_ref[...] = wg_ref[...].astype(wgb_ref.dtype)
        wub_ref[...] = wu_ref[...].astype(wub_ref.dtype)

    x = x_ref[...]
    o_ref[...] = (_silu(_dot(x, wgb_ref[...])) * _dot(x, wub_ref[...])).astype(o_ref.dtype)


def _swiglu(xb, wg, wu, layer, *, tm, tn):
    m, k = xb.shape
    d_ff = wg.shape[-1]
    wspec = pl.BlockSpec((None, k, tn), lambda j, i: (layer, 0, j))
    return pl.pallas_call(
        _swiglu_kernel,
        name="ffn_hidden",
        grid=(d_ff // tn, m // tm),
        in_specs=[pl.BlockSpec((tm, k), lambda j, i: (i, 0)), wspec, wspec],
        out_specs=pl.BlockSpec((tm, tn), lambda j, i: (i, j)),
        out_shape=jax.ShapeDtypeStruct((m, d_ff), MXU_DTYPE),
        scratch_shapes=[pltpu.VMEM((k, tn), MXU_DTYPE), pltpu.VMEM((k, tn), MXU_DTYPE)],
        compiler_params=_params("arbitrary", "arbitrary"),
    )(xb, wg, wu)


def _rope_kernel(pos_ref, freq_ref, sign_ref, cos_ref, sin_ref):
    ang = pos_ref[...].astype(F32) * freq_ref[...]
    cos_ref[...] = jnp.cos(ang)
    sin_ref[...] = jnp.sin(ang) * sign_ref[...]


def _rope_tables(positions, ts):
    m = positions.size
    half_h, half_i = B_HEAD_DIM // 2, IDX_DIM // 2
    lane = np.arange(LANES)
    inv_h = ROPE_THETA ** (-jnp.arange(half_h, dtype=F32) / half_h)
    inv_i = ROPE_THETA ** (-jnp.arange(half_i, dtype=F32) / half_i)
    freq = jnp.concatenate([jnp.tile(inv_h, LANES // half_h), jnp.tile(inv_i, LANES // half_i)])[None, :]
    s_h = np.where(lane % B_HEAD_DIM < half_h, -1.0, 1.0)
    s_i = np.where(lane % IDX_DIM < half_i, -1.0, 1.0)
    sign = jnp.asarray(np.concatenate([s_h, s_i])[None, :], F32)
    return pl.pallas_call(
        _rope_kernel,
        name="rope_tables",
        grid=(m // ts,),
        in_specs=[pl.BlockSpec((ts, 1), lambda i: (i, 0)),
                  pl.BlockSpec((1, 2 * LANES), lambda i: (0, 0)),
                  pl.BlockSpec((1, 2 * LANES), lambda i: (0, 0))],
        out_specs=[pl.BlockSpec((ts, 2 * LANES), lambda i: (i, 0))] * 2,
        out_shape=[jax.ShapeDtypeStruct((m, 2 * LANES), F32)] * 2,
        compiler_params=_params("arbitrary"),
    )(positions.reshape(m, 1), freq, sign)


def _hgrn_tables():
    c = CHUNK
    mall = np.zeros((N_LEVELS + 2, c, c), np.float32)
    masks = np.zeros((N_LEVELS + 1, c, c), np.float32)
    t = np.arange(c)
    mall[0] = (t[None, :] <= t[:, None])
    masks[0] = np.eye(c)
    for lvl in range(1, N_LEVELS + 1):
        half = c >> lvl
        mid = (t // (2 * half)) * (2 * half) + half - 1
        right = t > mid
        j = t[None, :]
        mall[lvl] = np.where(right[:, None], (j > mid[:, None]) & (j <= t[:, None]),
                             (j > t[:, None]) & (j <= mid[:, None]))
        same = (t[:, None] // (2 * half)) == (t[None, :] // (2 * half))
        masks[lvl] = same & right[:, None] & (~right[None, :])
    mall[N_LEVELS + 1] = (t[None, :] > t[:, None])
    return (jnp.asarray(mall.reshape(-1, c), MXU_DTYPE), jnp.asarray(masks, F32))


def _hgrn_kernel(aq_ref, af_ref, ai_ref, ag_ref, lbraw_ref, ng_ref, mall_ref, msk_ref,
                 o_ref, st_ref, e_ref, s_ref, oi_ref, *, layer, n_chunks):
    @pl.when(pl.program_id(1) == 0)
    def _():
        st_ref[...] = jnp.zeros_like(st_ref)

    raw = lbraw_ref[...]
    ex = jnp.exp(raw - jnp.max(raw, axis=0, keepdims=True))
    sm = ex / jnp.sum(ex, axis=0, keepdims=True)
    lb = jnp.zeros((1, A_WIDTH), F32)
    for j in range(1, layer + 1):
        lb = lb + sm[j:j + 1]
    ng = ng_ref[...]
    c = CHUNK

    def chunk_body(ci, carry):
        r0 = pl.multiple_of(ci * c, c)
        rows = pl.ds(r0, c)
        fp = af_ref[rows, :]
        logf = jnp.minimum(fp, 0.0) - jnp.log1p(jnp.exp(-jnp.abs(fp))) + jnp.log1p(lb * jnp.exp(-fp))
        kk = (1.0 - lb) * (1.0 / (1.0 + jnp.exp(fp)))
        qf = _silu(aq_ref[rows, :])
        vv = ai_ref[rows, :].astype(MXU_DTYPE)
        gate = _silu(ag_ref[rows, :])
        h1 = logf.astype(MXU_DTYPE)
        h2 = (logf - h1.astype(F32)).astype(MXU_DTYPE)
        mall = mall_ref[...]
        e_ref[...] = jnp.exp(_dot(mall, h1) + _dot(mall, h2))
        for h in range(A_HEADS):
            hs = slice(h * A_HEAD_DIM, (h + 1) * A_HEAD_DIM)
            qh, kh, vh = qf[:, hs], kk[:, hs], vv[:, hs]
            st = st_ref[h]
            oi_ref[:, hs] = _dot_nt((qh * e_ref[0:c, hs]).astype(MXU_DTYPE), st.astype(MXU_DTYPE))
            s = msk_ref[0] * _dot_nt(qh.astype(MXU_DTYPE), kh.astype(MXU_DTYPE))
            for lvl in range(1, N_LEVELS + 1):
                el = e_ref[lvl * c:(lvl + 1) * c, hs]
                s = s + msk_ref[lvl] * _dot_nt((qh * el).astype(MXU_DTYPE), (kh * el).astype(MXU_DTYPE))
            s_ref[h] = s.astype(s_ref.dtype)
            e_last = e_ref[(N_LEVELS + 1) * c:(N_LEVELS + 2) * c, hs]
            st_ref[h] = st * e_ref[c - 1:c, hs] + _dot_tn(vh, (kh * e_last).astype(MXU_DTYPE))
        for h in range(A_HEADS):
            hs = slice(h * A_HEAD_DIM, (h + 1) * A_HEAD_DIM)
            o = oi_ref[:, hs] + _dot(s_ref[h], vv[:, hs])
            o = o * lax.rsqrt(jnp.mean(o * o, axis=-1, keepdims=True) + RMS_EPS) * ng * gate[:, hs]
            o_ref[rows, hs] = o.astype(o_ref.dtype)
        return carry

    lax.fori_loop(0, n_chunks, chunk_body, 0)


def _hgrn(p1, lbraw, norm_g, layer, bsz, seq, tc):
    m = bsz * seq
    nt = seq // tc
    mall, masks = _hgrn_tables()
    depth = lbraw.shape[0]
    col = lambda cb: pl.BlockSpec((tc, A_WIDTH), lambda b, t, cb=cb: (b * nt + t, cb))
    return pl.pallas_call(
        functools.partial(_hgrn_kernel, layer=layer, n_chunks=tc // CHUNK),
        name="hgrn2",
        grid=(bsz, nt),
        in_specs=[col(0), col(1), col(2), col(3),
                  pl.BlockSpec((depth, A_WIDTH), lambda b, t: (0, 0)),
                  pl.BlockSpec((None, 1, A_HEAD_DIM), lambda b, t: (layer, 0, 0)),
                  pl.BlockSpec(mall.shape, lambda b, t: (0, 0)),
                  pl.BlockSpec(masks.shape, lambda b, t: (0, 0, 0))],
        out_specs=pl.BlockSpec((tc, A_WIDTH), lambda b, t: (b * nt + t, 0)),
        out_shape=jax.ShapeDtypeStruct((m, A_WIDTH), MXU_DTYPE),
        scratch_shapes=[pltpu.VMEM((A_HEADS, A_HEAD_DIM, A_HEAD_DIM), F32),
                        pltpu.VMEM(((N_LEVELS + 2) * CHUNK, A_WIDTH), F32),
                        pltpu.VMEM((A_HEADS, CHUNK, CHUNK), MXU_DTYPE),
                        pltpu.VMEM((CHUNK, A_WIDTH), F32)],
        compiler_params=_params("arbitrary", "arbitrary"),
    )(p1, p1, p1, p1, lbraw, norm_g, mall, masks)


def _pair_partner(x, lo):
    return jnp.where(lo, pltpu.roll(x, LANES - IDX_DIM // 2, 1), pltpu.roll(x, IDX_DIM // 2, 1))


def _prep_kernel(q_ref, k_ref, v_ref, iq_ref, kw_ref, cos_ref, sin_ref, kng_ref, knb_ref,
                 qr_ref, kr_ref, vt_ref, qir_ref, kiab_ref, wt_ref):
    c_h, s_h = cos_ref[:, 0:LANES], sin_ref[:, 0:LANES]
    c_i, s_i = cos_ref[:, LANES:2 * LANES], sin_ref[:, LANES:2 * LANES]
    for h in range(B_HEADS):
        hs = slice(h * B_HEAD_DIM, (h + 1) * B_HEAD_DIM)
        x = q_ref[:, hs]
        qr_ref[:, hs] = ((x * c_h + pltpu.roll(x, B_HEAD_DIM // 2, 1) * s_h) * Q_SCALE).astype(qr_ref.dtype)
        x = k_ref[:, hs]
        kr_ref[:, hs] = (x * c_h + pltpu.roll(x, B_HEAD_DIM // 2, 1) * s_h).astype(kr_ref.dtype)
    vt_ref[...] = v_ref[...].T.astype(vt_ref.dtype)
    lane = lax.broadcasted_iota(I32, (1, LANES), 1)
    lo = (lane % IDX_DIM) < (IDX_DIM // 2)
    for j in range(IDX_WIDTH // LANES):
        js = slice(j * LANES, (j + 1) * LANES)
        x = iq_ref[:, js]
        qir_ref[:, js] = (x * c_i + _pair_partner(x, lo) * s_i).astype(qir_ref.dtype)
    t = kw_ref[...]
    in_k = lane < IDX_DIM
    mu = jnp.sum(jnp.where(in_k, t, 0.0), axis=-1, keepdims=True) / IDX_DIM
    xc = jnp.where(in_k, t - mu, 0.0)
    var = jnp.sum(xc * xc, axis=-1, keepdims=True) / IDX_DIM
    kn = xc * lax.rsqrt(var + LN_EPS) * kng_ref[...] + knb_ref[...]
    ki = jnp.where(in_k, kn * c_i + _pair_partner(kn, lo) * s_i, 0.0)
    kiab_ref[:, 0:LANES] = ki.astype(kiab_ref.dtype)
    kiab_ref[:, LANES:2 * LANES] = pltpu.roll(ki, IDX_DIM, 1).astype(kiab_ref.dtype)
    wt_ref[...] = (t * IDX_SCALE).T


def _prep(p2, p3, cos_t, sin_t, kn_g, kn_b, layer, bsz, seq, ts):
    m = bsz * seq
    nt = seq // ts
    col = lambda cb: pl.BlockSpec((ts, B_WIDTH), lambda i, cb=cb: (i, cb))
    row = lambda w: pl.BlockSpec((ts, w), lambda i: (i, 0))
    vec = pl.BlockSpec((None, 1, LANES), lambda i: (layer, 0, 0))
    return pl.pallas_call(
        _prep_kernel,
        name="dsa_prep",
        grid=(m // ts,),
        in_specs=[col(0), col(1), col(2), col(3), row(LANES), row(2 * LANES), row(2 * LANES), vec, vec],
        out_specs=[row(B_WIDTH), row(B_WIDTH),
                   pl.BlockSpec((None, B_WIDTH, ts), lambda i: (i // nt, 0, i % nt)),
                   row(IDX_WIDTH), row(2 * LANES),
                   pl.BlockSpec((None, LANES, ts), lambda i: (i // nt, 0, i % nt))],
        out_shape=[jax.ShapeDtypeStruct((m, B_WIDTH), MXU_DTYPE),
                   jax.ShapeDtypeStruct((m, B_WIDTH), MXU_DTYPE),
                   jax.ShapeDtypeStruct((bsz, B_WIDTH, seq), MXU_DTYPE),
                   jax.ShapeDtypeStruct((m, IDX_WIDTH), MXU_DTYPE),
                   jax.ShapeDtypeStruct((m, 2 * LANES), MXU_DTYPE),
                   jax.ShapeDtypeStruct((bsz, LANES, seq), F32)],
        compiler_params=_params("arbitrary"),
    )(p2, p2, p2, p2, p3, cos_t, sin_t, kn_g, kn_b)


def _indexer_kernel(kiab_ref, qi_ref, wt_ref, o_ref, keys_ref, hi_ref, lo_ref, *, tq, topk, seq):
    qb = pl.program_id(1)
    n_kb = qb + 1
    t_idx = qb * tq + lax.broadcasted_iota(I32, (1, tq), 1)
    q_chunk = t_idx // CHUNK

    def key_rows(kb):
        r0 = pl.multiple_of(kb * tq, tq)
        return r0, r0 + lax.broadcasted_iota(I32, (tq, tq), 0)

    def score_body(kb, carry):
        r0, s_idx = key_rows(kb)
        ka = kiab_ref[pl.ds(r0, tq), 0:LANES]
        kb_ = kiab_ref[pl.ds(r0, tq), LANES:2 * LANES]
        acc = jnp.zeros((tq, tq), F32)
        for j in range(IDX_WIDTH // LANES):
            qt = qi_ref[:, j * LANES:(j + 1) * LANES]
            w0 = wt_ref[IDX_DIM + 2 * j:IDX_DIM + 2 * j + 1, :]
            w1 = wt_ref[IDX_DIM + 2 * j + 1:IDX_DIM + 2 * j + 2, :]
            acc = acc + jnp.maximum(_dot_nt(ka, qt), 0.0) * w0 + jnp.maximum(_dot_nt(kb_, qt), 0.0) * w1
        sc = jnp.where((s_idx // CHUNK) <= q_chunk, acc, MASK_VALUE)
        bits = lax.bitcast_convert_type(sc, I32)
        key = bits ^ ((bits >> 31) & 0x7FFFFFFF)
        keys_ref[pl.ds(r0, tq), :] = key
        hi_ref[pl.ds(r0, tq), :] = (key >> 16).astype(I16)
        return carry

    lax.fori_loop(0, n_kb, score_body, 0)

    def count(pred):
        def body(kb, c8):
            r0, s_idx = key_rows(kb)
            p = pred(keys_ref[pl.ds(r0, tq), :], s_idx)
            return c8 + jnp.sum(jnp.where(p, 1, 0).astype(I32).reshape(tq // 8, 8, tq), axis=0)
        c8 = lax.fori_loop(0, n_kb, body, jnp.zeros((8, tq), I32))
        return jnp.sum(c8, axis=0, keepdims=True)

    def count16(ref, pred):
        def body(kb, c16):
            r0 = pl.multiple_of(kb * tq, tq)
            one = jnp.where(pred(ref[pl.ds(r0, tq), :]), jnp.int16(1), jnp.int16(0))
            for r in range(tq // 16):
                c16 = c16 + one[r * 16:(r + 1) * 16, :]
            return c16
        c16 = lax.fori_loop(0, n_kb, body, jnp.zeros((16, tq), I16))
        return jnp.sum(c16.astype(I32), axis=0, keepdims=True)

    def bisect16(ref, rank):
        def bit_body(i, v):
            cand = jnp.where(i == 0, 0, v | lax.shift_left(jnp.int32(1), 15 - i))
            cand16 = cand.astype(I16)
            c = count16(ref, lambda x: x >= cand16)
            return jnp.where(c >= rank, cand, v)
        return lax.fori_loop(0, 16, bit_body, jnp.full((1, tq), -2 ** 15, I32))

    hi = bisect16(hi_ref, topk)
    hi16 = hi.astype(I16)
    rank_lo = topk - count16(hi_ref, lambda x: x > hi16)

    def low_body(kb, carry):
        r0 = pl.multiple_of(kb * tq, tq)
        x = keys_ref[pl.ds(r0, tq), :]
        lo = jnp.where((x >> 16) == hi, (x & 0xFFFF) - 2 ** 15, -2 ** 15)
        lo_ref[pl.ds(r0, tq), :] = lo.astype(I16)
        return carry

    lax.fori_loop(0, n_kb, low_body, 0)
    lo = bisect16(lo_ref, rank_lo)
    thr = (hi << 16) | (lo + 2 ** 15)
    c_ge = count(lambda x, s_idx: x >= thr)
    tie = (c_ge > topk) & (thr > KEY_MASK)
    any_tie = jnp.max(jnp.where(tie, 1, 0)) > 0

    @pl.when(any_tie)
    def _():
        need = topk - count(lambda x, s_idx: x > thr)

        def jbit_body(i, jj):
            cand = jj | lax.shift_left(jnp.int32(1), (seq - 1).bit_length() - 1 - i)
            c = count(lambda x, s_idx: (x == thr) & (s_idx < cand))
            return jnp.where(c < need, cand, jj)

        jj = lax.fori_loop(0, (seq - 1).bit_length(), jbit_body, jnp.zeros((1, tq), I32))
        j_lim = jnp.where(tie, jj, seq)

        def out_body(kb, carry):
            r0, s_idx = key_rows(kb)
            x = keys_ref[pl.ds(r0, tq), :]
            sel = ((x > thr) | ((x == thr) & (s_idx <= j_lim))) & ((s_idx // CHUNK) <= q_chunk)
            o_ref[pl.ds(r0, tq), :] = jnp.where(sel, 0.0, MASK_VALUE).astype(o_ref.dtype)
            return carry

        lax.fori_loop(0, n_kb, out_body, 0)

    @pl.when(jnp.logical_not(any_tie))
    def _():
        def out_body(kb, carry):
            r0 = pl.multiple_of(kb * tq, tq)
            x = keys_ref[pl.ds(r0, tq), :]
            o_ref[pl.ds(r0, tq), :] = jnp.where(x >= thr, 0.0, MASK_VALUE).astype(o_ref.dtype)
            return carry

        lax.fori_loop(0, qb, out_body, 0)
        r0, s_idx = key_rows(qb)
        x = keys_ref[pl.ds(r0, tq), :]
        sel = jnp.where((s_idx // CHUNK) <= q_chunk, jnp.where(x >= thr, 0.0, MASK_VALUE), MASK_VALUE)
        o_ref[pl.ds(r0, tq), :] = sel.astype(o_ref.dtype)

    def fill_body(kb, carry):
        r0 = pl.multiple_of(kb * tq, tq)
        o_ref[pl.ds(r0, tq), :] = jnp.full((tq, tq), MASK_VALUE, o_ref.dtype)
        return carry

    lax.fori_loop(n_kb, seq // tq, fill_body, 0)


def _indexer(kiab, qir, wt, bsz, seq, tq, topk):
    nq = seq // tq
    return pl.pallas_call(
        functools.partial(_indexer_kernel, tq=tq, topk=topk, seq=seq),
        name="dsa_indexer",
        grid=(bsz, nq),
        in_specs=[pl.BlockSpec((seq, 2 * LANES), lambda b, q: (b, 0)),
                  pl.BlockSpec((tq, IDX_WIDTH), lambda b, q: (b * nq + q, 0)),
                  pl.BlockSpec((None, LANES, tq), lambda b, q: (b, 0, q))],
        out_specs=pl.BlockSpec((None, seq, tq), lambda b, q: (b, 0, q)),
        out_shape=jax.ShapeDtypeStruct((bsz, seq, seq), MXU_DTYPE),
        scratch_shapes=[pltpu.VMEM((seq, tq), I32), pltpu.VMEM((seq, tq), I16), pltpu.VMEM((seq, tq), I16)],
        compiler_params=_params("arbitrary", "arbitrary"),
    )(kiab, qir, wt)


def _attn_kernel(q_ref, k_ref, vt_ref, bias_ref, o_ref, m_ref, l_ref, acc_ref, lg_ref, *, tq):
    qb = pl.program_id(1)
    n_kb = qb + 1
    m_ref[...] = jnp.full(m_ref.shape, -jnp.inf, F32)
    l_ref[...] = jnp.zeros(l_ref.shape, F32)
    acc_ref[...] = jnp.zeros(acc_ref.shape, F32)

    def body(kb, carry):
        r0 = pl.multiple_of(kb * tq, tq)
        bias = bias_ref[pl.ds(r0, tq), :].astype(F32)
        for h in range(B_HEADS):
            hs = slice(h * B_HEAD_DIM, (h + 1) * B_HEAD_DIM)
            lg_ref[h] = _dot_nt(k_ref[pl.ds(r0, tq), hs], q_ref[:, hs]) + bias
        for h in range(B_HEADS):
            hs = slice(h * B_HEAD_DIM, (h + 1) * B_HEAD_DIM)
            lg = lg_ref[h]
            m_i = m_ref[h]
            m_new = jnp.maximum(m_i, jnp.max(lg, axis=0, keepdims=True))
            alpha = jnp.exp2(m_i - m_new)
            p = jnp.exp2(lg - m_new)
            l_ref[h] = alpha * l_ref[h] + jnp.sum(p, axis=0, keepdims=True)
            m_ref[h] = m_new
            acc_ref[h] = alpha * acc_ref[h] + _dot(vt_ref[hs, pl.ds(r0, tq)], p.astype(MXU_DTYPE))
        return carry

    lax.fori_loop(0, n_kb, body, 0)
    for h in range(B_HEADS):
        hs = slice(h * B_HEAD_DIM, (h + 1) * B_HEAD_DIM)
        o_ref[:, hs] = (acc_ref[h] / l_ref[h]).T.astype(o_ref.dtype)


def _attention(qr, kr, vt, bias, bsz, seq, tq):
    nq = seq // tq
    m = bsz * seq
    return pl.pallas_call(
        functools.partial(_attn_kernel, tq=tq),
        name="dsa_attention",
        grid=(bsz, nq),
        in_specs=[pl.BlockSpec((tq, B_WIDTH), lambda b, q: (b * nq + q, 0)),
                  _resident((seq, B_WIDTH), lambda b, q: (b, 0)),
                  _resident((None, B_WIDTH, seq), lambda b, q: (b, 0, 0)),
                  pl.BlockSpec((None, seq, tq), lambda b, q: (b, 0, q))],
        out_specs=pl.BlockSpec((tq, B_WIDTH), lambda b, q: (b * nq + q, 0)),
        out_shape=jax.ShapeDtypeStruct((m, B_WIDTH), MXU_DTYPE),
        scratch_shapes=[pltpu.VMEM((B_HEADS, 1, tq), F32), pltpu.VMEM((B_HEADS, 1, tq), F32),
                        pltpu.VMEM((B_HEADS, B_HEAD_DIM, tq), F32),
                        pltpu.VMEM((B_HEADS, tq, tq), F32)],
        compiler_params=_params("arbitrary", "arbitrary"),
    )(qr, kr, vt, bias)


def _mix_kernel(ya_ref, yb_ref, ga_ref, gb_ref, x_ref, wa_ref, wb_ref, wo_ref, g_ref, b_ref,
                xo_ref, xbo_ref, *, alpha):
    merged = ga_ref[...].astype(F32) * _dot(ya_ref[...], wa_ref[...]) \
        + gb_ref[...].astype(F32) * _dot(yb_ref[...], wb_ref[...])
    z = alpha * x_ref[...] + _dot(merged.astype(MXU_DTYPE), wo_ref[...])
    xn = _layer_norm(z, g_ref[...], b_ref[...])
    xo_ref[...] = xn
    xbo_ref[...] = xn.astype(xbo_ref.dtype)


def _mix(ya, yb, gates, x, wa, wb, wo, g, b, layer, alpha, tm):
    m, d = x.shape
    row = lambda w: pl.BlockSpec((tm, w), lambda i: (i, 0))
    vec = pl.BlockSpec((None, 1, d), lambda i: (layer, 0, 0))
    return pl.pallas_call(
        functools.partial(_mix_kernel, alpha=alpha),
        name="mix_out_ln",
        grid=(m // tm,),
        in_specs=[row(A_WIDTH), row(B_WIDTH),
                  pl.BlockSpec((tm, d), lambda i: (i, 0)), pl.BlockSpec((tm, d), lambda i: (i, 1)),
                  row(d),
                  _resident((None, A_WIDTH, d), lambda i: (layer, 0, 0)),
                  _resident((None, B_WIDTH, d), lambda i: (layer, 0, 0)),
                  _resident((None, d, d), lambda i: (layer, 0, 0)),
                  vec, vec],
        out_specs=[row(d), row(d)],
        out_shape=[jax.ShapeDtypeStruct((m, d), F32), jax.ShapeDtypeStruct((m, d), MXU_DTYPE)],
        compiler_params=_params("arbitrary"),
    )(ya, yb, gates, gates, x, wa, wb, wo, g, b)


def _ffn_down_kernel(h_ref, x_ref, wd_ref, g_ref, b_ref, xo_ref, xbo_ref, *, alpha):
    z = alpha * x_ref[...] + _dot(h_ref[...], wd_ref[...])
    xn = _layer_norm(z, g_ref[...], b_ref[...])
    xo_ref[...] = xn
    xbo_ref[...] = xn.astype(xbo_ref.dtype)


def _ffn_down(hid, x, wd, g, b, layer, alpha, tm):
    m, d = x.shape
    d_ff = hid.shape[-1]
    row = pl.BlockSpec((tm, d), lambda i: (i, 0))
    vec = pl.BlockSpec((None, 1, d), lambda i: (layer, 0, 0))
    return pl.pallas_call(
        functools.partial(_ffn_down_kernel, alpha=alpha),
        name="ffn_down_ln",
        grid=(m // tm,),
        in_specs=[pl.BlockSpec((tm, d_ff), lambda i: (i, 0)), row,
                  _resident((None, d_ff, d), lambda i: (layer, 0, 0)), vec, vec],
        out_specs=[row, row],
        out_shape=[jax.ShapeDtypeStruct((m, d), F32), jax.ShapeDtypeStruct((m, d), MXU_DTYPE)],
        compiler_params=_params("arbitrary"),
    )(hid, x, wd, g, b)


def _ple_kernel(xb_ref, x_ref, p_ref, wg_ref, wp_ref, g_ref, b_ref, xo_ref, xbo_ref, *, alpha):
    ple = _sigmoid(_dot(xb_ref[...], wg_ref[...])) * _dot(p_ref[...].astype(MXU_DTYPE), wp_ref[...])
    xn = _layer_norm(alpha * x_ref[...] + ple, g_ref[...], b_ref[...])
    xo_ref[...] = xn
    xbo_ref[...] = xn.astype(xbo_ref.dtype)


def _ple(xb, x, p, wg, wp, g, b, layer, alpha, tm):
    m, d = x.shape
    pd = p.shape[-1]
    row = pl.BlockSpec((tm, d), lambda i: (i, 0))
    vec = pl.BlockSpec((None, 1, d), lambda i: (layer, 0, 0))
    return pl.pallas_call(
        functools.partial(_ple_kernel, alpha=alpha),
        name="ple_ln",
        grid=(m // tm,),
        in_specs=[row, row,
                  pl.BlockSpec((None, tm, pd), lambda i: (layer, i, 0)),
                  _resident((None, d, d), lambda i: (layer, 0, 0)),
                  _resident((None, pd, d), lambda i: (layer, 0, 0)),
                  vec, vec],
        out_specs=[row, row],
        out_shape=[jax.ShapeDtypeStruct((m, d), F32), jax.ShapeDtypeStruct((m, d), MXU_DTYPE)],
        compiler_params=_params("arbitrary"),
    )(xb, x, p, wg, wp, g, b)


def kernel(x, p, positions, w_in, w_branch_a, w_branch_b, w_out, hgrn_lower_bounds, hgrn_norm_g,
           idx_k_norm_g, idx_k_norm_b, ln_mix_g, ln_mix_b, w_ffn_gate, w_ffn_up, w_ffn_down,
           ln_ffn_g, ln_ffn_b, w_ple_gate, w_ple_proj, ln_ple_g, ln_ple_b):
    bsz, seq, d = x.shape
    depth = w_in.shape[0]
    m = bsz * seq
    alpha = (2 * depth) ** 0.25
    topk = min(TOPK_MAX, seq // 4)
    tq = 256
    assert seq % tq == 0 and topk <= tq and tq % CHUNK == 0
    tm = min(1024, m)
    ts = min(512, seq)
    gate_col0 = 4 * A_WIDTH + 3 * B_WIDTH + IDX_WIDTH + IDX_DIM + IDX_HEADS
    assert w_in.shape[-1] == gate_col0 + 2 * d

    cast = lambda w: w.astype(MXU_DTYPE)
    w_in_f = w_in.astype(F32)
    w_gates = w_in_f[:, :, gate_col0:]
    wa, wb, wo = cast(w_branch_a), cast(w_branch_b), cast(w_out)
    wg, wu, wd = w_ffn_gate.astype(F32), w_ffn_up.astype(F32), cast(w_ffn_down)
    wpg, wpp = cast(w_ple_gate), cast(w_ple_proj)
    vec3 = lambda v: v.astype(F32).reshape(depth, 1, -1)
    pad_idx = lambda v: jnp.pad(v.astype(F32), ((0, 0), (0, LANES - IDX_DIM))).reshape(depth, 1, LANES)
    kn_g, kn_b = pad_idx(idx_k_norm_g), pad_idx(idx_k_norm_b)
    norm_g = vec3(hgrn_norm_g)
    lbraw = hgrn_lower_bounds.astype(F32)
    ln = [(vec3(g_), vec3(b_)) for g_, b_ in
          ((ln_mix_g, ln_mix_b), (ln_ffn_g, ln_ffn_b), (ln_ple_g, ln_ple_b))]
    p2d = p.reshape(depth, m, p.shape[-1])

    cos_t, sin_t = _rope_tables(positions, ts)
    xf = x.reshape(m, d).astype(F32)
    xb = xf.astype(MXU_DTYPE)
    for layer in range(depth):
        p1 = _matmul(xb, w_in_f, layer, 0, 4 * A_WIDTH, tm=tm, tn=1024, out_dtype=F32)
        p2 = _matmul(xb, w_in_f, layer, 4 * A_WIDTH // 1024, 3 * B_WIDTH + IDX_WIDTH,
                     tm=tm, tn=1024, out_dtype=F32)
        p3 = _matmul(xb, w_in_f, layer, (4 * A_WIDTH + 3 * B_WIDTH + IDX_WIDTH) // LANES, LANES,
                     tm=tm, tn=LANES, out_dtype=F32)
        gates = _matmul(xb, w_gates, layer, 0, 2 * d, tm=tm, tn=1024, out_dtype=MXU_DTYPE, act="sigmoid")
        ya = _hgrn(p1, lbraw, norm_g, layer, bsz, seq, min(256, seq))
        qr, kr, vt, qir, kiab, wt = _prep(p2, p3, cos_t, sin_t, kn_g, kn_b, layer, bsz, seq, ts)
        bias = _indexer(kiab, qir, wt, bsz, seq, tq, topk)
        yb = _attention(qr, kr, vt, bias, bsz, seq, tq)
        xf, xb = _mix(ya, yb, gates, xf, wa, wb, wo, ln[0][0], ln[0][1], layer, alpha, min(256, m))
        hid = _swiglu(xb, wg, wu, layer, tm=tm, tn=512)
        xf, xb = _ffn_down(hid, xf, wd, ln[1][0], ln[1][1], layer, alpha, min(256, m))
        xf, xb = _ple(xb, xf, p2d, wpg, wpp, ln[2][0], ln[2][1], layer, alpha, min(256, m))
    return xf.reshape(bsz, seq, d).astype(x.dtype)
```

```python
import functools

import numpy as np
import jax
import jax.numpy as jnp
from jax import lax
from jax.experimental import pallas as pl
from jax.experimental.pallas import tpu as pltpu

F32 = jnp.float32
I32 = jnp.int32
I16 = jnp.int16
MXU_DTYPE = jnp.bfloat16

CHUNK = 64
A_HEADS = 8
A_HEAD_DIM = 128
B_HEADS = 8
B_HEAD_DIM = 128
IDX_HEADS = 16
IDX_DIM = 64
TOPK_MAX = 256
ROPE_THETA = 10000.0
LN_EPS = 1e-5
RMS_EPS = 1e-6
MASK_VALUE = -1e30
ATTN_SCALE = B_HEAD_DIM ** -0.5
Q_SCALE = ATTN_SCALE * 1.4426950408889634
IDX_SCALE = (IDX_HEADS * IDX_DIM) ** -0.5
A_WIDTH = A_HEADS * A_HEAD_DIM
B_WIDTH = B_HEADS * B_HEAD_DIM
IDX_WIDTH = IDX_HEADS * IDX_DIM
LANES = 128
N_LEVELS = 6
VMEM_LIMIT = 56 * 1024 * 1024

_INT_MIN = -2 ** 31


def _float_key(v):
    bits = int(np.float32(v).view(np.int32))
    return bits ^ ((bits >> 31) & 0x7FFFFFFF)


KEY_MASK = _float_key(MASK_VALUE)


def _dot(a, b):
    return jnp.dot(a, b, preferred_element_type=F32)


def _dot_nt(a, b):
    return lax.dot_general(a, b, (((1,), (1,)), ((), ())), preferred_element_type=F32)


def _dot_tn(a, b):
    return lax.dot_general(a, b, (((0,), (0,)), ((), ())), preferred_element_type=F32)


def _sigmoid(x):
    return 1.0 / (1.0 + jnp.exp(-x))


def _silu(x):
    return x * _sigmoid(x)


def _layer_norm(z, g, b):
    mu = jnp.mean(z, axis=-1, keepdims=True)
    zc = z - mu
    var = jnp.mean(zc * zc, axis=-1, keepdims=True)
    return zc * lax.rsqrt(var + LN_EPS) * g + b


def _params(*sem, flags=None):
    return pltpu.CompilerParams(dimension_semantics=sem, vmem_limit_bytes=VMEM_LIMIT, flags=flags)


def _resident(block_shape, index_map):
    return pl.BlockSpec(block_shape, index_map, pipeline_mode=pl.Buffered(1))


def _mm_kernel(x_ref, w_ref, o_ref, wb_ref, *, act):
    @pl.when(pl.program_id(1) == 0)
    def _():
        wb_ref[...] = w_ref[...].astype(wb_ref.dtype)

    acc = _dot(x_ref[...], wb_ref[...])
    if act == "sigmoid":
        acc = _sigmoid(acc)
    o_ref[...] = acc.astype(o_ref.dtype)


def _matmul(xb, w, layer, col_blk0, ncols, *, tm, tn, out_dtype, act=None):
    m, k = xb.shape
    return pl.pallas_call(
        functools.partial(_mm_kernel, act=act),
        name="proj_" + (act or "linear"),
        grid=(ncols // tn, m // tm),
        in_specs=[pl.BlockSpec((tm, k), lambda j, i: (i, 0)),
                  pl.BlockSpec((None, k, tn), lambda j, i: (layer, 0, col_blk0 + j))],
        out_specs=pl.BlockSpec((tm, tn), lambda j, i: (i, j)),
        out_shape=jax.ShapeDtypeStruct((m, ncols), out_dtype),
        scratch_shapes=[pltpu.VMEM((k, tn), MXU_DTYPE)],
        compiler_params=_params("arbitrary", "arbitrary"),
    )(xb, w)


def _swiglu_kernel(x_ref, wg_ref, wu_ref, o_ref, wgb_ref, wub_ref):
    @pl.when(pl.program_id(1) == 0)
    def _():
        wgb_ref[...] = wg_ref[...].astype(wgb_ref.dtype)
        wub_ref[...] = wu_ref[...].astype(wub_ref.dtype)

    x = x_ref[...]
    o_ref[...] = (_silu(_dot(x, wgb_ref[...])) * _dot(x, wub_ref[...])).astype(o_ref.dtype)


def _swiglu(xb, wg, wu, layer, *, tm, tn):
    m, k = xb.shape
    d_ff = wg.shape[-1]
    wspec = pl.BlockSpec((None, k, tn), lambda j, i: (layer, 0, j))
    return pl.pallas_call(
        _swiglu_kernel,
        name="ffn_hidden",
        grid=(d_ff // tn, m // tm),
        in_specs=[pl.BlockSpec((tm, k), lambda j, i: (i, 0)), wspec, wspec],
        out_specs=pl.BlockSpec((tm, tn), lambda j, i: (i, j)),
        out_shape=jax.ShapeDtypeStruct((m, d_ff), MXU_DTYPE),
        scratch_shapes=[pltpu.VMEM((k, tn), MXU_DTYPE), pltpu.VMEM((k, tn), MXU_DTYPE)],
        compiler_params=_params("arbitrary", "arbitrary"),
    )(xb, wg, wu)


def _rope_kernel(pos_ref, freq_ref, sign_ref, cos_ref, sin_ref):
    ang = pos_ref[...].astype(F32) * freq_ref[...]
    cos_ref[...] = jnp.cos(ang)
    sin_ref[...] = jnp.sin(ang) * sign_ref[...]


def _rope_tables(positions, ts):
    m = positions.size
    half_h, half_i = B_HEAD_DIM // 2, IDX_DIM // 2
    lane = np.arange(LANES)
    inv_h = ROPE_THETA ** (-jnp.arange(half_h, dtype=F32) / half_h)
    inv_i = ROPE_THETA ** (-jnp.arange(half_i, dtype=F32) / half_i)
    freq = jnp.concatenate([jnp.tile(inv_h, LANES // half_h), jnp.tile(inv_i, LANES // half_i)])[None, :]
    s_h = np.where(lane % B_HEAD_DIM < half_h, -1.0, 1.0)
    s_i = np.where(lane % IDX_DIM < half_i, -1.0, 1.0)
    sign = jnp.asarray(np.concatenate([s_h, s_i])[None, :], F32)
    return pl.pallas_call(
        _rope_kernel,
        name="rope_tables",
        grid=(m // ts,),
        in_specs=[pl.BlockSpec((ts, 1), lambda i: (i, 0)),
                  pl.BlockSpec((1, 2 * LANES), lambda i: (0, 0)),
                  pl.BlockSpec((1, 2 * LANES), lambda i: (0, 0))],
        out_specs=[pl.BlockSpec((ts, 2 * LANES), lambda i: (i, 0))] * 2,
        out_shape=[jax.ShapeDtypeStruct((m, 2 * LANES), F32)] * 2,
        compiler_params=_params("arbitrary"),
    )(positions.reshape(m, 1), freq, sign)


def _hgrn_tables():
    c = CHUNK
    mall = np.zeros((N_LEVELS + 2, c, c), np.float32)
    masks = np.zeros((N_LEVELS + 1, c, c), np.float32)
    t = np.arange(c)
    mall[0] = (t[None, :] <= t[:, None])
    masks[0] = np.eye(c)
    for lvl in range(1, N_LEVELS + 1):
        half = c >> lvl
        mid = (t // (2 * half)) * (2 * half) + half - 1
        right = t > mid
        j = t[None, :]
        mall[lvl] = np.where(right[:, None], (j > mid[:, None]) & (j <= t[:, None]),
                             (j > t[:, None]) & (j <= mid[:, None]))
        same = (t[:, None] // (2 * half)) == (t[None, :] // (2 * half))
        masks[lvl] = same & right[:, None] & (~right[None, :])
    mall[N_LEVELS + 1] = (t[None, :] > t[:, None])
    return (jnp.asarray(mall.reshape(-1, c), MXU_DTYPE), jnp.asarray(masks, F32))


def _hgrn_kernel(aq_ref, af_ref, ai_ref, ag_ref, lbraw_ref, ng_ref, mall_ref, msk_ref,
                 o_ref, st_ref, e_ref, s_ref, oi_ref, *, layer, n_chunks):
    @pl.when(pl.program_id(1) == 0)
    def _():
        st_ref[...] = jnp.zeros_like(st_ref)

    raw = lbraw_ref[...]
    ex = jnp.exp(raw - jnp.max(raw, axis=0, keepdims=True))
    sm = ex / jnp.sum(ex, axis=0, keepdims=True)
    lb = jnp.zeros((1, A_WIDTH), F32)
    for j in range(1, layer + 1):
        lb = lb + sm[j:j + 1]
    ng = ng_ref[...]
    c = CHUNK

    def chunk_body(ci, carry):
        r0 = pl.multiple_of(ci * c, c)
        rows = pl.ds(r0, c)
        fp = af_ref[rows, :]
        logf = jnp.minimum(fp, 0.0) - jnp.log1p(jnp.exp(-jnp.abs(fp))) + jnp.log1p(lb * jnp.exp(-fp))
        kk = (1.0 - lb) * (1.0 / (1.0 + jnp.exp(fp)))
        qf = _silu(aq_ref[rows, :])
        vv = ai_ref[rows, :].astype(MXU_DTYPE)
        gate = _silu(ag_ref[rows, :])
        h1 = logf.astype(MXU_DTYPE)
        h2 = (logf - h1.astype(F32)).astype(MXU_DTYPE)
        mall = mall_ref[...]
        e_ref[...] = jnp.exp(_dot(mall, h1) + _dot(mall, h2))
        for h in range(A_HEADS):
            hs = slice(h * A_HEAD_DIM, (h + 1) * A_HEAD_DIM)
            qh, kh, vh = qf[:, hs], kk[:, hs], vv[:, hs]
            st = st_ref[h]
            oi_ref[:, hs] = _dot_nt((qh * e_ref[0:c, hs]).astype(MXU_DTYPE), st.astype(MXU_DTYPE))
            s = msk_ref[0] * _dot_nt(qh.astype(MXU_DTYPE), kh.astype(MXU_DTYPE))
            for lvl in range(1, N_LEVELS + 1):
                el = e_ref[lvl * c:(lvl + 1) * c, hs]
                s = s + msk_ref[lvl] * _dot_nt((qh * el).astype(MXU_DTYPE), (kh * el).astype(MXU_DTYPE))
            s_ref[h] = s.astype(s_ref.dtype)
            e_last = e_ref[(N_LEVELS + 1) * c:(N_LEVELS + 2) * c, hs]
            st_ref[h] = st * e_ref[c - 1:c, hs] + _dot_tn(vh, (kh * e_last).astype(MXU_DTYPE))
        for h in range(A_HEADS):
            hs = slice(h * A_HEAD_DIM, (h + 1) * A_HEAD_DIM)
            o = oi_ref[:, hs] + _dot(s_ref[h], vv[:, hs])
            o = o * lax.rsqrt(jnp.mean(o * o, axis=-1, keepdims=True) + RMS_EPS) * ng * gate[:, hs]
            o_ref[rows, hs] = o.astype(o_ref.dtype)
        return carry

    lax.fori_loop(0, n_chunks, chunk_body, 0)


def _hgrn(p1, lbraw, norm_g, layer, bsz, seq, tc):
    m = bsz * seq
    nt = seq // tc
    mall, masks = _hgrn_tables()
    depth = lbraw.shape[0]
    col = lambda cb: pl.BlockSpec((tc, A_WIDTH), lambda b, t, cb=cb: (b * nt + t, cb))
    return pl.pallas_call(
        functools.partial(_hgrn_kernel, layer=layer, n_chunks=tc // CHUNK),
        name="hgrn2",
        grid=(bsz, nt),
        in_specs=[col(0), col(1), col(2), col(3),
                  pl.BlockSpec((depth, A_WIDTH), lambda b, t: (0, 0)),
                  pl.BlockSpec((None, 1, A_HEAD_DIM), lambda b, t: (layer, 0, 0)),
                  pl.BlockSpec(mall.shape, lambda b, t: (0, 0)),
                  pl.BlockSpec(masks.shape, lambda b, t: (0, 0, 0))],
        out_specs=pl.BlockSpec((tc, A_WIDTH), lambda b, t: (b * nt + t, 0)),
        out_shape=jax.ShapeDtypeStruct((m, A_WIDTH), MXU_DTYPE),
        scratch_shapes=[pltpu.VMEM((A_HEADS, A_HEAD_DIM, A_HEAD_DIM), F32),
                        pltpu.VMEM(((N_LEVELS + 2) * CHUNK, A_WIDTH), F32),
                        pltpu.VMEM((A_HEADS, CHUNK, CHUNK), MXU_DTYPE),
                        pltpu.VMEM((CHUNK, A_WIDTH), F32)],
        compiler_params=_params("arbitrary", "arbitrary"),
    )(p1, p1, p1, p1, lbraw, norm_g, mall, masks)


def _pair_partner(x, lo):
    return jnp.where(lo, pltpu.roll(x, LANES - IDX_DIM // 2, 1), pltpu.roll(x, IDX_DIM // 2, 1))


def _dsa_proj_kernel(*refs, mode):
    if mode == "ikw":
        x_ref, w_ref, cos_ref, sin_ref, kng_ref, knb_ref, kiab_ref, wt_ref, wb_ref = refs
    else:
        x_ref, w_ref, cos_ref, sin_ref, o_ref, wb_ref = refs

    @pl.when(pl.program_id(1) == 0)
    def _():
        wb_ref[...] = w_ref[...].astype(wb_ref.dtype)

    acc = _dot(x_ref[...], wb_ref[...])
    lane = lax.broadcasted_iota(I32, (1, LANES), 1)
    lo = (lane % IDX_DIM) < (IDX_DIM // 2)
    c_i, s_i = cos_ref[:, LANES:2 * LANES], sin_ref[:, LANES:2 * LANES]
    if mode == "qk":
        c_h, s_h = cos_ref[:, 0:LANES], sin_ref[:, 0:LANES]
        scale = jnp.where(pl.program_id(0) == 0, Q_SCALE, 1.0).astype(F32)
        for h in range(acc.shape[1] // B_HEAD_DIM):
            hs = slice(h * B_HEAD_DIM, (h + 1) * B_HEAD_DIM)
            x = acc[:, hs]
            o_ref[:, hs] = ((x * c_h + pltpu.roll(x, B_HEAD_DIM // 2, 1) * s_h) * scale).astype(o_ref.dtype)
    elif mode == "v":
        o_ref[...] = acc.T.astype(o_ref.dtype)
    elif mode == "iq":
        for j in range(acc.shape[1] // LANES):
            js = slice(j * LANES, (j + 1) * LANES)
            x = acc[:, js]
            o_ref[:, js] = (x * c_i + _pair_partner(x, lo) * s_i).astype(o_ref.dtype)
    else:
        in_k = lane < IDX_DIM
        mu = jnp.sum(jnp.where(in_k, acc, 0.0), axis=-1, keepdims=True) / IDX_DIM
        xc = jnp.where(in_k, acc - mu, 0.0)
        var = jnp.sum(xc * xc, axis=-1, keepdims=True) / IDX_DIM
        kn = xc * lax.rsqrt(var + LN_EPS) * kng_ref[...] + knb_ref[...]
        ki = jnp.where(in_k, kn * c_i + _pair_partner(kn, lo) * s_i, 0.0)
        kiab_ref[:, 0:LANES] = ki.astype(kiab_ref.dtype)
        kiab_ref[:, LANES:2 * LANES] = pltpu.roll(ki, IDX_DIM, 1).astype(kiab_ref.dtype)
        wt_ref[...] = (acc * IDX_SCALE).T


def _dsa_proj(xb, w, cos_t, sin_t, kn_g, kn_b, layer, bsz, seq, tm, mode):
    m, k = xb.shape
    nt = seq // tm
    col0 = 4 * A_WIDTH
    tn, ncols, cb0 = {"qk": (B_WIDTH, 2 * B_WIDTH, col0 // B_WIDTH),
                      "v": (B_WIDTH, B_WIDTH, (col0 + 2 * B_WIDTH) // B_WIDTH),
                      "iq": (IDX_WIDTH, IDX_WIDTH, (col0 + 3 * B_WIDTH) // IDX_WIDTH),
                      "ikw": (LANES, LANES, (col0 + 3 * B_WIDTH + IDX_WIDTH) // LANES)}[mode]
    rows = lambda w_: pl.BlockSpec((tm, w_), lambda j, i: (i, 0))
    in_specs = [rows(k), pl.BlockSpec((None, k, tn), lambda j, i: (layer, 0, cb0 + j)),
                rows(2 * LANES), rows(2 * LANES)]
    args = [xb, w, cos_t, sin_t]
    transposed = lambda h: pl.BlockSpec((None, h, tm), lambda j, i: (i // nt, 0, i % nt))
    if mode == "ikw":
        vec = pl.BlockSpec((None, 1, LANES), lambda j, i: (layer, 0, 0))
        in_specs += [vec, vec]
        args += [kn_g, kn_b]
        out_specs = [rows(2 * LANES), transposed(LANES)]
        out_shape = [jax.ShapeDtypeStruct((m, 2 * LANES), MXU_DTYPE),
                     jax.ShapeDtypeStruct((bsz, LANES, seq), F32)]
    elif mode == "v":
        out_specs = transposed(B_WIDTH)
        out_shape = jax.ShapeDtypeStruct((bsz, B_WIDTH, seq), MXU_DTYPE)
    else:
        out_specs = pl.BlockSpec((tm, tn), lambda j, i: (i, j))
        out_shape = jax.ShapeDtypeStruct((m, ncols), MXU_DTYPE)
    return pl.pallas_call(
        functools.partial(_dsa_proj_kernel, mode=mode),
        name="dsa_proj_" + mode,
        grid=(ncols // tn, m // tm),
        in_specs=in_specs,
        out_specs=out_specs,
        out_shape=out_shape,
        scratch_shapes=[pltpu.VMEM((k, tn), MXU_DTYPE)],
        compiler_params=_params("arbitrary", "arbitrary"),
    )(*args)


def _indexer_kernel(kiab_ref, qi_ref, wt_ref, o_ref, keys_ref, hi_ref, lo_ref, *, tq, topk, seq):
    qb = pl.program_id(1)
    n_kb = qb + 1
    t_idx = qb * tq + lax.broadcasted_iota(I32, (1, tq), 1)
    q_chunk = t_idx // CHUNK

    def key_rows(kb):
        r0 = pl.multiple_of(kb * tq, tq)
        return r0, r0 + lax.broadcasted_iota(I32, (tq, tq), 0)

    def score_body(kb, carry):
        r0, s_idx = key_rows(kb)
        ka = kiab_ref[pl.ds(r0, tq), 0:LANES]
        kb_ = kiab_ref[pl.ds(r0, tq), LANES:2 * LANES]
        acc = jnp.zeros((tq, tq), F32)
        for j in range(IDX_WIDTH // LANES):
            qt = qi_ref[:, j * LANES:(j + 1) * LANES]
            w0 = wt_ref[IDX_DIM + 2 * j:IDX_DIM + 2 * j + 1, :]
            w1 = wt_ref[IDX_DIM + 2 * j + 1:IDX_DIM + 2 * j + 2, :]
            acc = acc + jnp.maximum(_dot_nt(ka, qt), 0.0) * w0 + jnp.maximum(_dot_nt(kb_, qt), 0.0) * w1
        sc = jnp.where((s_idx // CHUNK) <= q_chunk, acc, MASK_VALUE)
        bits = lax.bitcast_convert_type(sc, I32)
        key = bits ^ ((bits >> 31) & 0x7FFFFFFF)
        keys_ref[pl.ds(r0, tq), :] = key
        hi_ref[pl.ds(r0, tq), :] = (key >> 16).astype(I16)
        return carry

    lax.fori_loop(0, n_kb, score_body, 0)

    def count(pred):
        def body(kb, c8):
            r0, s_idx = key_rows(kb)
            p = pred(keys_ref[pl.ds(r0, tq), :], s_idx)
            return c8 + jnp.sum(jnp.where(p, 1, 0).astype(I32).reshape(tq // 8, 8, tq), axis=0)
        c8 = lax.fori_loop(0, n_kb, body, jnp.zeros((8, tq), I32))
        return jnp.sum(c8, axis=0, keepdims=True)

    def count16(ref, pred):
        def body(kb, c16):
            r0 = pl.multiple_of(kb * tq, tq)
            one = jnp.where(pred(ref[pl.ds(r0, tq), :]), jnp.int16(1), jnp.int16(0))
            for r in range(tq // 16):
                c16 = c16 + one[r * 16:(r + 1) * 16, :]
            return c16
        c16 = lax.fori_loop(0, n_kb, body, jnp.zeros((16, tq), I16))
        return jnp.sum(c16.astype(I32), axis=0, keepdims=True)

    def bisect16(ref, rank):
        def bit_body(i, v):
            cand = jnp.where(i == 0, 0, v | lax.shift_left(jnp.int32(1), 15 - i))
            cand16 = cand.astype(I16)
            c = count16(ref, lambda x: x >= cand16)
            return jnp.where(c >= rank, cand, v)
        return lax.fori_loop(0, 16, bit_body, jnp.full((1, tq), -2 ** 15, I32))

    hi = bisect16(hi_ref, topk)
    hi16 = hi.astype(I16)
    rank_lo = topk - count16(hi_ref, lambda x: x > hi16)

    def low_body(kb, carry):
        r0 = pl.multiple_of(kb * tq, tq)
        x = keys_ref[pl.ds(r0, tq), :]
        lo = jnp.where((x >> 16) == hi, (x & 0xFFFF) - 2 ** 15, -2 ** 15)
        lo_ref[pl.ds(r0, tq), :] = lo.astype(I16)
        return carry

    lax.fori_loop(0, n_kb, low_body, 0)
    lo = bisect16(lo_ref, rank_lo)
    thr = (hi << 16) | (lo + 2 ** 15)
    c_ge = count(lambda x, s_idx: x >= thr)
    tie = (c_ge > topk) & (thr > KEY_MASK)
    any_tie = jnp.max(jnp.where(tie, 1, 0)) > 0

    @pl.when(any_tie)
    def _():
        need = topk - count(lambda x, s_idx: x > thr)

        def jbit_body(i, jj):
            cand = jj | lax.shift_left(jnp.int32(1), (seq - 1).bit_length() - 1 - i)
            c = count(lambda x, s_idx: (x == thr) & (s_idx < cand))
            return jnp.where(c < need, cand, jj)

        jj = lax.fori_loop(0, (seq - 1).bit_length(), jbit_body, jnp.zeros((1, tq), I32))
        j_lim = jnp.where(tie, jj, seq)

        def out_body(kb, carry):
            r0, s_idx = key_rows(kb)
            x = keys_ref[pl.ds(r0, tq), :]
            sel = ((x > thr) | ((x == thr) & (s_idx <= j_lim))) & ((s_idx // CHUNK) <= q_chunk)
            o_ref[pl.ds(r0, tq), :] = jnp.where(sel, 0.0, MASK_VALUE).astype(o_ref.dtype)
            return carry

        lax.fori_loop(0, n_kb, out_body, 0)

    @pl.when(jnp.logical_not(any_tie))
    def _():
        def out_body(kb, carry):
            r0 = pl.multiple_of(kb * tq, tq)
            x = keys_ref[pl.ds(r0, tq), :]
            o_ref[pl.ds(r0, tq), :] = jnp.where(x >= thr, 0.0, MASK_VALUE).astype(o_ref.dtype)
            return carry

        lax.fori_loop(0, qb, out_body, 0)
        r0, s_idx = key_rows(qb)
        x = keys_ref[pl.ds(r0, tq), :]
        sel = jnp.where((s_idx // CHUNK) <= q_chunk, jnp.where(x >= thr, 0.0, MASK_VALUE), MASK_VALUE)
        o_ref[pl.ds(r0, tq), :] = sel.astype(o_ref.dtype)

    def fill_body(kb, carry):
        r0 = pl.multiple_of(kb * tq, tq)
        o_ref[pl.ds(r0, tq), :] = jnp.full((tq, tq), MASK_VALUE, o_ref.dtype)
        return carry

    lax.fori_loop(n_kb, seq // tq, fill_body, 0)


def _indexer(kiab, qir, wt, bsz, seq, tq, topk):
    nq = seq // tq
    return pl.pallas_call(
        functools.partial(_indexer_kernel, tq=tq, topk=topk, seq=seq),
        name="dsa_indexer",
        grid=(bsz, nq),
        in_specs=[pl.BlockSpec((seq, 2 * LANES), lambda b, q: (b, 0)),
                  pl.BlockSpec((tq, IDX_WIDTH), lambda b, q: (b * nq + q, 0)),
                  pl.BlockSpec((None, LANES, tq), lambda b, q: (b, 0, q))],
        out_specs=pl.BlockSpec((None, seq, tq), lambda b, q: (b, 0, q)),
        out_shape=jax.ShapeDtypeStruct((bsz, seq, seq), MXU_DTYPE),
        scratch_shapes=[pltpu.VMEM((seq, tq), I32), pltpu.VMEM((seq, tq), I16), pltpu.VMEM((seq, tq), I16)],
        compiler_params=_params("arbitrary", "arbitrary"),
    )(kiab, qir, wt)


def _attn_kernel(q_ref, k_ref, vt_ref, bias_ref, o_ref, m_ref, l_ref, acc_ref, lg_ref, *, tq):
    qb = pl.program_id(1)
    n_kb = qb + 1
    m_ref[...] = jnp.full(m_ref.shape, -jnp.inf, F32)
    l_ref[...] = jnp.zeros(l_ref.shape, F32)
    acc_ref[...] = jnp.zeros(acc_ref.shape, F32)

    def body(kb, carry):
        r0 = pl.multiple_of(kb * tq, tq)
        bias = bias_ref[pl.ds(r0, tq), :].astype(F32)
        for h in range(B_HEADS):
            hs = slice(h * B_HEAD_DIM, (h + 1) * B_HEAD_DIM)
            lg_ref[h] = _dot_nt(k_ref[pl.ds(r0, tq), hs], q_ref[:, hs]) + bias
        for h in range(B_HEADS):
            hs = slice(h * B_HEAD_DIM, (h + 1) * B_HEAD_DIM)
            lg = lg_ref[h]
            m_i = m_ref[h]
            m_new = jnp.maximum(m_i, jnp.max(lg, axis=0, keepdims=True))
            alpha = jnp.exp2(m_i - m_new)
            p = jnp.exp2(lg - m_new)
            l_ref[h] = alpha * l_ref[h] + jnp.sum(p, axis=0, keepdims=True)
            m_ref[h] = m_new
            acc_ref[h] = alpha * acc_ref[h] + _dot(vt_ref[hs, pl.ds(r0, tq)], p.astype(MXU_DTYPE))
        return carry

    lax.fori_loop(0, n_kb, body, 0)
    for h in range(B_HEADS):
        hs = slice(h * B_HEAD_DIM, (h + 1) * B_HEAD_DIM)
        o_ref[:, hs] = (acc_ref[h] / l_ref[h]).T.astype(o_ref.dtype)


def _attention(qk, vt, bias, bsz, seq, tq):
    nq = seq // tq
    m = bsz * seq
    return pl.pallas_call(
        functools.partial(_attn_kernel, tq=tq),
        name="dsa_attention",
        grid=(bsz, nq),
        in_specs=[pl.BlockSpec((tq, B_WIDTH), lambda b, q: (b * nq + q, 0)),
                  _resident((seq, B_WIDTH), lambda b, q: (b, 1)),
                  _resident((None, B_WIDTH, seq), lambda b, q: (b, 0, 0)),
                  pl.BlockSpec((None, seq, tq), lambda b, q: (b, 0, q))],
        out_specs=pl.BlockSpec((tq, B_WIDTH), lambda b, q: (b * nq + q, 0)),
        out_shape=jax.ShapeDtypeStruct((m, B_WIDTH), MXU_DTYPE),
        scratch_shapes=[pltpu.VMEM((B_HEADS, 1, tq), F32), pltpu.VMEM((B_HEADS, 1, tq), F32),
                        pltpu.VMEM((B_HEADS, B_HEAD_DIM, tq), F32),
                        pltpu.VMEM((B_HEADS, tq, tq), F32)],
        compiler_params=_params("arbitrary", "arbitrary"),
    )(qk, qk, vt, bias)


def _mix_kernel(ya_ref, yb_ref, ga_ref, gb_ref, x_ref, wa_ref, wb_ref, wo_ref, g_ref, b_ref,
                xo_ref, xbo_ref, *, alpha):
    merged = ga_ref[...].astype(F32) * _dot(ya_ref[...], wa_ref[...]) \
        + gb_ref[...].astype(F32) * _dot(yb_ref[...], wb_ref[...])
    z = alpha * x_ref[...] + _dot(merged.astype(MXU_DTYPE), wo_ref[...])
    xn = _layer_norm(z, g_ref[...], b_ref[...])
    xo_ref[...] = xn
    xbo_ref[...] = xn.astype(xbo_ref.dtype)


def _mix(ya, yb, gates, x, wa, wb, wo, g, b, layer, alpha, tm):
    m, d = x.shape
    row = lambda w: pl.BlockSpec((tm, w), lambda i: (i, 0))
    vec = pl.BlockSpec((None, 1, d), lambda i: (layer, 0, 0))
    return pl.pallas_call(
        functools.partial(_mix_kernel, alpha=alpha),
        name="mix_out_ln",
        grid=(m // tm,),
        in_specs=[row(A_WIDTH), row(B_WIDTH),
                  pl.BlockSpec((tm, d), lambda i: (i, 0)), pl.BlockSpec((tm, d), lambda i: (i, 1)),
                  row(d),
                  _resident((None, A_WIDTH, d), lambda i: (layer, 0, 0)),
                  _resident((None, B_WIDTH, d), lambda i: (layer, 0, 0)),
                  _resident((None, d, d), lambda i: (layer, 0, 0)),
                  vec, vec],
        out_specs=[row(d), row(d)],
        out_shape=[jax.ShapeDtypeStruct((m, d), F32), jax.ShapeDtypeStruct((m, d), MXU_DTYPE)],
        compiler_params=_params("arbitrary"),
    )(ya, yb, gates, gates, x, wa, wb, wo, g, b)


def _ffn_down_kernel(h_ref, x_ref, wd_ref, g_ref, b_ref, xo_ref, xbo_ref, *, alpha):
    z = alpha * x_ref[...] + _dot(h_ref[...], wd_ref[...])
    xn = _layer_norm(z, g_ref[...], b_ref[...])
    xo_ref[...] = xn
    xbo_ref[...] = xn.astype(xbo_ref.dtype)


def _ffn_down(hid, x, wd, g, b, layer, alpha, tm):
    m, d = x.shape
    d_ff = hid.shape[-1]
    row = pl.BlockSpec((tm, d), lambda i: (i, 0))
    vec = pl.BlockSpec((None, 1, d), lambda i: (layer, 0, 0))
    return pl.pallas_call(
        functools.partial(_ffn_down_kernel, alpha=alpha),
        name="ffn_down_ln",
        grid=(m // tm,),
        in_specs=[pl.BlockSpec((tm, d_ff), lambda i: (i, 0)), row,
                  _resident((None, d_ff, d), lambda i: (layer, 0, 0)), vec, vec],
        out_specs=[row, row],
        out_shape=[jax.ShapeDtypeStruct((m, d), F32), jax.ShapeDtypeStruct((m, d), MXU_DTYPE)],
        compiler_params=_params("arbitrary"),
    )(hid, x, wd, g, b)


def _ple_kernel(xb_ref, x_ref, p_ref, wg_ref, wp_ref, g_ref, b_ref, xo_ref, xbo_ref, *, alpha):
    ple = _sigmoid(_dot(xb_ref[...], wg_ref[...])) * _dot(p_ref[...].astype(MXU_DTYPE), wp_ref[...])
    xn = _layer_norm(alpha * x_ref[...] + ple, g_ref[...], b_ref[...])
    xo_ref[...] = xn
    xbo_ref[...] = xn.astype(xbo_ref.dtype)


def _ple(xb, x, p, wg, wp, g, b, layer, alpha, tm):
    m, d = x.shape
    pd = p.shape[-1]
    row = pl.BlockSpec((tm, d), lambda i: (i, 0))
    vec = pl.BlockSpec((None, 1, d), lambda i: (layer, 0, 0))
    return pl.pallas_call(
        functools.partial(_ple_kernel, alpha=alpha),
        name="ple_ln",
        grid=(m // tm,),
        in_specs=[row, row,
                  pl.BlockSpec((None, tm, pd), lambda i: (layer, i, 0)),
                  _resident((None, d, d), lambda i: (layer, 0, 0)),
                  _resident((None, pd, d), lambda i: (layer, 0, 0)),
                  vec, vec],
        out_specs=[row, row],
        out_shape=[jax.ShapeDtypeStruct((m, d), F32), jax.ShapeDtypeStruct((m, d), MXU_DTYPE)],
        compiler_params=_params("arbitrary"),
    )(xb, x, p, wg, wp, g, b)


def kernel(x, p, positions, w_in, w_branch_a, w_branch_b, w_out, hgrn_lower_bounds, hgrn_norm_g,
           idx_k_norm_g, idx_k_norm_b, ln_mix_g, ln_mix_b, w_ffn_gate, w_ffn_up, w_ffn_down,
           ln_ffn_g, ln_ffn_b, w_ple_gate, w_ple_proj, ln_ple_g, ln_ple_b):
    bsz, seq, d = x.shape
    depth = w_in.shape[0]
    m = bsz * seq
    alpha = (2 * depth) ** 0.25
    topk = min(TOPK_MAX, seq // 4)
    tq = 256
    assert seq % tq == 0 and topk <= tq and tq % CHUNK == 0
    tm = min(1024, m)
    ts = min(512, seq)
    gate_col0 = 4 * A_WIDTH + 3 * B_WIDTH + IDX_WIDTH + IDX_DIM + IDX_HEADS
    assert w_in.shape[-1] == gate_col0 + 2 * d

    cast = lambda w: w.astype(MXU_DTYPE)
    w_in_f = w_in.astype(F32)
    w_gates = w_in_f[:, :, gate_col0:]
    wa, wb, wo = cast(w_branch_a), cast(w_branch_b), cast(w_out)
    wg, wu, wd = w_ffn_gate.astype(F32), w_ffn_up.astype(F32), cast(w_ffn_down)
    wpg, wpp = cast(w_ple_gate), cast(w_ple_proj)
    vec3 = lambda v: v.astype(F32).reshape(depth, 1, -1)
    pad_idx = lambda v: jnp.pad(v.astype(F32), ((0, 0), (0, LANES - IDX_DIM))).reshape(depth, 1, LANES)
    kn_g, kn_b = pad_idx(idx_k_norm_g), pad_idx(idx_k_norm_b)
    norm_g = vec3(hgrn_norm_g)
    lbraw = hgrn_lower_bounds.astype(F32)
    ln = [(vec3(g_), vec3(b_)) for g_, b_ in
          ((ln_mix_g, ln_mix_b), (ln_ffn_g, ln_ffn_b), (ln_ple_g, ln_ple_b))]
    p2d = p.reshape(depth, m, p.shape[-1])

    cos_t, sin_t = _rope_tables(positions, ts)
    xf = x.reshape(m, d).astype(F32)
    xb = xf.astype(MXU_DTYPE)
    for layer in range(depth):
        p1 = _matmul(xb, w_in_f, layer, 0, 4 * A_WIDTH, tm=tm, tn=1024, out_dtype=F32)
        dsa = [_dsa_proj(xb, w_in_f, cos_t, sin_t, kn_g, kn_b, layer, bsz, seq, min(tm, seq), mode)
               for mode in ("qk", "v", "iq", "ikw")]
        qk, vt, qir, (kiab, wt) = dsa
        gates = _matmul(xb, w_gates, layer, 0, 2 * d, tm=tm, tn=1024, out_dtype=MXU_DTYPE, act="sigmoid")
        ya = _hgrn(p1, lbraw, norm_g, layer, bsz, seq, min(256, seq))
        bias = _indexer(kiab, qir, wt, bsz, seq, tq, topk)
        yb = _attention(qk, vt, bias, bsz, seq, tq)
        xf, xb = _mix(ya, yb, gates, xf, wa, wb, wo, ln[0][0], ln[0][1], layer, alpha, min(256, m))
        hid = _swiglu(xb, wg, wu, layer, tm=tm, tn=512)
        xf, xb = _ffn_down(hid, xf, wd, ln[1][0], ln[1][1], layer, alpha, min(256, m))
        xf, xb = _ple(xb, xf, p2d, wpg, wpp, ln[2][0], ln[2][1], layer, alpha, min(256, m))
    return xf.reshape(bsz, seq, d).astype(x.dtype)
```

```python
import functools

import numpy as np
import jax
import jax.numpy as jnp
from jax import lax
from jax.experimental import pallas as pl
from jax.experimental.pallas import tpu as pltpu

F32 = jnp.float32
I32 = jnp.int32
I16 = jnp.int16
MXU_DTYPE = jnp.bfloat16

CHUNK = 64
A_HEADS = 8
A_HEAD_DIM = 128
B_HEADS = 8
B_HEAD_DIM = 128
IDX_HEADS = 16
IDX_DIM = 64
TOPK_MAX = 256
ROPE_THETA = 10000.0
LN_EPS = 1e-5
RMS_EPS = 1e-6
MASK_VALUE = -1e30
ATTN_SCALE = B_HEAD_DIM ** -0.5
Q_SCALE = ATTN_SCALE * 1.4426950408889634
IDX_SCALE = (IDX_HEADS * IDX_DIM) ** -0.5
A_WIDTH = A_HEADS * A_HEAD_DIM
B_WIDTH = B_HEADS * B_HEAD_DIM
IDX_WIDTH = IDX_HEADS * IDX_DIM
LANES = 128
N_LEVELS = 6
VMEM_LIMIT = 56 * 1024 * 1024

_INT_MIN = -2 ** 31


def _float_key(v):
    bits = int(np.float32(v).view(np.int32))
    return bits ^ ((bits >> 31) & 0x7FFFFFFF)


KEY_MASK = _float_key(MASK_VALUE)


def _dot(a, b):
    return jnp.dot(a, b, preferred_element_type=F32)


def _dot_nt(a, b):
    return lax.dot_general(a, b, (((1,), (1,)), ((), ())), preferred_element_type=F32)


def _dot_tn(a, b):
    return lax.dot_general(a, b, (((0,), (0,)), ((), ())), preferred_element_type=F32)


def _sigmoid(x):
    return 1.0 / (1.0 + jnp.exp(-x))


def _silu(x):
    return x * _sigmoid(x)


def _layer_norm(z, g, b):
    mu = jnp.mean(z, axis=-1, keepdims=True)
    zc = z - mu
    var = jnp.mean(zc * zc, axis=-1, keepdims=True)
    return zc * lax.rsqrt(var + LN_EPS) * g + b


def _params(*sem, flags=None):
    return pltpu.CompilerParams(dimension_semantics=sem, vmem_limit_bytes=VMEM_LIMIT, flags=flags)


def _resident(block_shape, index_map):
    return pl.BlockSpec(block_shape, index_map, pipeline_mode=pl.Buffered(1))


def _mm_kernel(x_ref, w_ref, o_ref, *, act):
    acc = _dot(x_ref[...], w_ref[...])
    if act == "sigmoid":
        acc = _sigmoid(acc)
    o_ref[...] = acc.astype(o_ref.dtype)


def _matmul(xb, w, layer, col_blk0, ncols, *, tm, tn, out_dtype, act=None):
    m, k = xb.shape
    return pl.pallas_call(
        functools.partial(_mm_kernel, act=act),
        name="proj_" + (act or "linear"),
        grid=(ncols // tn, m // tm),
        in_specs=[pl.BlockSpec((tm, k), lambda j, i: (i, 0)),
                  pl.BlockSpec((None, k, tn), lambda j, i: (layer, 0, col_blk0 + j))],
        out_specs=pl.BlockSpec((tm, tn), lambda j, i: (i, j)),
        out_shape=jax.ShapeDtypeStruct((m, ncols), out_dtype),
        compiler_params=_params("arbitrary", "arbitrary"),
    )(xb, w)


def _swiglu_kernel(x_ref, wg_ref, wu_ref, o_ref, wgb_ref, wub_ref):
    @pl.when(pl.program_id(1) == 0)
    def _():
        wgb_ref[...] = wg_ref[...].astype(wgb_ref.dtype)
        wub_ref[...] = wu_ref[...].astype(wub_ref.dtype)

    x = x_ref[...]
    o_ref[...] = (_silu(_dot(x, wgb_ref[...])) * _dot(x, wub_ref[...])).astype(o_ref.dtype)


def _swiglu(xb, wg, wu, layer, *, tm, tn):
    m, k = xb.shape
    d_ff = wg.shape[-1]
    wspec = pl.BlockSpec((None, k, tn), lambda j, i: (layer, 0, j))
    return pl.pallas_call(
        _swiglu_kernel,
        name="ffn_hidden",
        grid=(d_ff // tn, m // tm),
        in_specs=[pl.BlockSpec((tm, k), lambda j, i: (i, 0)), wspec, wspec],
        out_specs=pl.BlockSpec((tm, tn), lambda j, i: (i, j)),
        out_shape=jax.ShapeDtypeStruct((m, d_ff), MXU_DTYPE),
        scratch_shapes=[pltpu.VMEM((k, tn), MXU_DTYPE), pltpu.VMEM((k, tn), MXU_DTYPE)],
        compiler_params=_params("arbitrary", "arbitrary"),
    )(xb, wg, wu)


def _rope_kernel(pos_ref, freq_ref, sign_ref, cos_ref, sin_ref):
    ang = pos_ref[...].astype(F32) * freq_ref[...]
    cos_ref[...] = jnp.cos(ang)
    sin_ref[...] = jnp.sin(ang) * sign_ref[...]


def _rope_tables(positions, ts):
    m = positions.size
    half_h, half_i = B_HEAD_DIM // 2, IDX_DIM // 2
    lane = np.arange(LANES)
    inv_h = ROPE_THETA ** (-jnp.arange(half_h, dtype=F32) / half_h)
    inv_i = ROPE_THETA ** (-jnp.arange(half_i, dtype=F32) / half_i)
    freq = jnp.concatenate([jnp.tile(inv_h, LANES // half_h), jnp.tile(inv_i, LANES // half_i)])[None, :]
    s_h = np.where(lane % B_HEAD_DIM < half_h, -1.0, 1.0)
    s_i = np.where(lane % IDX_DIM < half_i, -1.0, 1.0)
    sign = jnp.asarray(np.concatenate([s_h, s_i])[None, :], F32)
    return pl.pallas_call(
        _rope_kernel,
        name="rope_tables",
        grid=(m // ts,),
        in_specs=[pl.BlockSpec((ts, 1), lambda i: (i, 0)),
                  pl.BlockSpec((1, 2 * LANES), lambda i: (0, 0)),
                  pl.BlockSpec((1, 2 * LANES), lambda i: (0, 0))],
        out_specs=[pl.BlockSpec((ts, 2 * LANES), lambda i: (i, 0))] * 2,
        out_shape=[jax.ShapeDtypeStruct((m, 2 * LANES), F32)] * 2,
        compiler_params=_params("arbitrary"),
    )(positions.reshape(m, 1), freq, sign)


def _hgrn_tables():
    c = CHUNK
    mall = np.zeros((N_LEVELS + 2, c, c), np.float32)
    masks = np.zeros((N_LEVELS + 1, c, c), np.float32)
    t = np.arange(c)
    mall[0] = (t[None, :] <= t[:, None])
    masks[0] = np.eye(c)
    for lvl in range(1, N_LEVELS + 1):
        half = c >> lvl
        mid = (t // (2 * half)) * (2 * half) + half - 1
        right = t > mid
        j = t[None, :]
        mall[lvl] = np.where(right[:, None], (j > mid[:, None]) & (j <= t[:, None]),
                             (j > t[:, None]) & (j <= mid[:, None]))
        same = (t[:, None] // (2 * half)) == (t[None, :] // (2 * half))
        masks[lvl] = same & right[:, None] & (~right[None, :])
    mall[N_LEVELS + 1] = (t[None, :] > t[:, None])
    return (jnp.asarray(mall.reshape(-1, c), MXU_DTYPE), jnp.asarray(masks, F32))


def _hgrn_kernel(aq_ref, af_ref, ai_ref, ag_ref, lbraw_ref, ng_ref, mall_ref, msk_ref,
                 o_ref, st_ref, e_ref, s_ref, oi_ref, *, layer, n_chunks, unroll):
    @pl.when(pl.program_id(1) == 0)
    def _():
        st_ref[...] = jnp.zeros_like(st_ref)

    raw = lbraw_ref[...]
    ex = jnp.exp(raw - jnp.max(raw, axis=0, keepdims=True))
    sm = ex / jnp.sum(ex, axis=0, keepdims=True)
    lb = jnp.zeros((1, A_WIDTH), F32)
    for j in range(1, layer + 1):
        lb = lb + sm[j:j + 1]
    ng = ng_ref[...]
    c = CHUNK

    def prepare(ci, u):
        rows = pl.ds(pl.multiple_of(ci * c, c), c)
        fp = af_ref[rows, :]
        logf = jnp.minimum(fp, 0.0) - jnp.log1p(jnp.exp(-jnp.abs(fp))) + jnp.log1p(lb * jnp.exp(-fp))
        kk = (1.0 - lb) * (1.0 / (1.0 + jnp.exp(fp)))
        qf = _silu(aq_ref[rows, :])
        vv = ai_ref[rows, :].astype(MXU_DTYPE)
        gate = _silu(ag_ref[rows, :])
        h1 = logf.astype(MXU_DTYPE)
        h2 = (logf - h1.astype(F32)).astype(MXU_DTYPE)
        mall = mall_ref[...]
        e_ref[u] = jnp.exp(_dot(mall, h1) + _dot(mall, h2))
        return rows, qf, kk, vv, gate

    def scores_and_state(u, qf, kk, vv):
        e_u = e_ref.at[u]
        for h in range(A_HEADS):
            hs = slice(h * A_HEAD_DIM, (h + 1) * A_HEAD_DIM)
            qh, kh, vh = qf[:, hs], kk[:, hs], vv[:, hs]
            st = st_ref[h]
            oi_ref[u, :, hs] = _dot_nt((qh * e_u[0:c, hs]).astype(MXU_DTYPE), st.astype(MXU_DTYPE))
            s = msk_ref[0] * _dot_nt(qh.astype(MXU_DTYPE), kh.astype(MXU_DTYPE))
            for lvl in range(1, N_LEVELS + 1):
                el = e_u[lvl * c:(lvl + 1) * c, hs]
                s = s + msk_ref[lvl] * _dot_nt((qh * el).astype(MXU_DTYPE), (kh * el).astype(MXU_DTYPE))
            s_ref[u * A_HEADS + h] = s.astype(s_ref.dtype)
            e_last = e_u[(N_LEVELS + 1) * c:(N_LEVELS + 2) * c, hs]
            st_ref[h] = st * e_u[c - 1:c, hs] + _dot_tn(vh, (kh * e_last).astype(MXU_DTYPE))

    def outputs(u, rows, vv, gate):
        for h in range(A_HEADS):
            hs = slice(h * A_HEAD_DIM, (h + 1) * A_HEAD_DIM)
            o = oi_ref[u, :, hs] + _dot(s_ref[u * A_HEADS + h], vv[:, hs])
            o = o * lax.rsqrt(jnp.mean(o * o, axis=-1, keepdims=True) + RMS_EPS) * ng * gate[:, hs]
            o_ref[rows, hs] = o.astype(o_ref.dtype)

    def step_body(si, carry):
        prep = [prepare(si * unroll + u, u) for u in range(unroll)]
        for u, (rows, qf, kk, vv, gate) in enumerate(prep):
            scores_and_state(u, qf, kk, vv)
        for u, (rows, qf, kk, vv, gate) in enumerate(prep):
            outputs(u, rows, vv, gate)
        return carry

    lax.fori_loop(0, n_chunks // unroll, step_body, 0)


def _hgrn(p1, lbraw, norm_g, layer, bsz, seq, tc):
    m = bsz * seq
    nt = seq // tc
    mall, masks = _hgrn_tables()
    depth = lbraw.shape[0]
    n_chunks = tc // CHUNK
    unroll = 2 if n_chunks % 2 == 0 else 1
    col = lambda cb: pl.BlockSpec((tc, A_WIDTH), lambda b, t, cb=cb: (b * nt + t, cb))
    return pl.pallas_call(
        functools.partial(_hgrn_kernel, layer=layer, n_chunks=n_chunks, unroll=unroll),
        name="hgrn2",
        grid=(bsz, nt),
        in_specs=[col(0), col(1), col(2), col(3),
                  pl.BlockSpec((depth, A_WIDTH), lambda b, t: (0, 0)),
                  pl.BlockSpec((None, 1, A_HEAD_DIM), lambda b, t: (layer, 0, 0)),
                  pl.BlockSpec(mall.shape, lambda b, t: (0, 0)),
                  pl.BlockSpec(masks.shape, lambda b, t: (0, 0, 0))],
        out_specs=pl.BlockSpec((tc, A_WIDTH), lambda b, t: (b * nt + t, 0)),
        out_shape=jax.ShapeDtypeStruct((m, A_WIDTH), MXU_DTYPE),
        scratch_shapes=[pltpu.VMEM((A_HEADS, A_HEAD_DIM, A_HEAD_DIM), F32),
                        pltpu.VMEM((unroll, (N_LEVELS + 2) * CHUNK, A_WIDTH), F32),
                        pltpu.VMEM((unroll * A_HEADS, CHUNK, CHUNK), MXU_DTYPE),
                        pltpu.VMEM((unroll, CHUNK, A_WIDTH), F32)],
        compiler_params=_params("arbitrary", "arbitrary"),
    )(p1, p1, p1, p1, lbraw, norm_g, mall, masks)


def _pair_partner(x, lo):
    return jnp.where(lo, pltpu.roll(x, LANES - IDX_DIM // 2, 1), pltpu.roll(x, IDX_DIM // 2, 1))


def _dsa_proj_kernel(*refs, mode):
    if mode == "ikw":
        x_ref, w_ref, cos_ref, sin_ref, kng_ref, knb_ref, kiab_ref, wt_ref = refs
    else:
        x_ref, w_ref, cos_ref, sin_ref, o_ref = refs
    acc = _dot(x_ref[...], w_ref[...])
    lane = lax.broadcasted_iota(I32, (1, LANES), 1)
    lo = (lane % IDX_DIM) < (IDX_DIM // 2)
    c_i, s_i = cos_ref[:, LANES:2 * LANES], sin_ref[:, LANES:2 * LANES]
    if mode == "qk":
        c_h, s_h = cos_ref[:, 0:LANES], sin_ref[:, 0:LANES]
        scale = jnp.where(pl.program_id(0) == 0, Q_SCALE, 1.0).astype(F32)
        for h in range(acc.shape[1] // B_HEAD_DIM):
            hs = slice(h * B_HEAD_DIM, (h + 1) * B_HEAD_DIM)
            x = acc[:, hs]
            o_ref[:, hs] = ((x * c_h + pltpu.roll(x, B_HEAD_DIM // 2, 1) * s_h) * scale).astype(o_ref.dtype)
    elif mode == "v":
        o_ref[...] = acc.T.astype(o_ref.dtype)
    elif mode == "iq":
        for j in range(acc.shape[1] // LANES):
            js = slice(j * LANES, (j + 1) * LANES)
            x = acc[:, js]
            o_ref[:, js] = (x * c_i + _pair_partner(x, lo) * s_i).astype(o_ref.dtype)
    else:
        in_k = lane < IDX_DIM
        mu = jnp.sum(jnp.where(in_k, acc, 0.0), axis=-1, keepdims=True) / IDX_DIM
        xc = jnp.where(in_k, acc - mu, 0.0)
        var = jnp.sum(xc * xc, axis=-1, keepdims=True) / IDX_DIM
        kn = xc * lax.rsqrt(var + LN_EPS) * kng_ref[...] + knb_ref[...]
        ki = jnp.where(in_k, kn * c_i + _pair_partner(kn, lo) * s_i, 0.0)
        kiab_ref[:, 0:LANES] = ki.astype(kiab_ref.dtype)
        kiab_ref[:, LANES:2 * LANES] = pltpu.roll(ki, IDX_DIM, 1).astype(kiab_ref.dtype)
        wt_ref[...] = (acc * IDX_SCALE).T


def _dsa_proj(xb, w, cos_t, sin_t, kn_g, kn_b, layer, bsz, seq, tm, mode):
    m, k = xb.shape
    nt = seq // tm
    tn, ncols, cb0 = {"qk": (B_WIDTH, 2 * B_WIDTH, 0),
                      "v": (B_WIDTH, B_WIDTH, 2 * B_WIDTH // B_WIDTH),
                      "iq": (IDX_WIDTH, IDX_WIDTH, 3 * B_WIDTH // IDX_WIDTH),
                      "ikw": (LANES, LANES, (3 * B_WIDTH + IDX_WIDTH) // LANES)}[mode]
    rows = lambda w_: pl.BlockSpec((tm, w_), lambda j, i: (i, 0))
    in_specs = [rows(k), pl.BlockSpec((None, k, tn), lambda j, i: (layer, 0, cb0 + j)),
                rows(2 * LANES), rows(2 * LANES)]
    args = [xb, w, cos_t, sin_t]
    transposed = lambda h: pl.BlockSpec((None, h, tm), lambda j, i: (i // nt, 0, i % nt))
    if mode == "ikw":
        vec = pl.BlockSpec((None, 1, LANES), lambda j, i: (layer, 0, 0))
        in_specs += [vec, vec]
        args += [kn_g, kn_b]
        out_specs = [rows(2 * LANES), transposed(LANES)]
        out_shape = [jax.ShapeDtypeStruct((m, 2 * LANES), MXU_DTYPE),
                     jax.ShapeDtypeStruct((bsz, LANES, seq), F32)]
    elif mode == "v":
        out_specs = transposed(B_WIDTH)
        out_shape = jax.ShapeDtypeStruct((bsz, B_WIDTH, seq), MXU_DTYPE)
    else:
        out_specs = pl.BlockSpec((tm, tn), lambda j, i: (i, j))
        out_shape = jax.ShapeDtypeStruct((m, ncols), MXU_DTYPE)
    return pl.pallas_call(
        functools.partial(_dsa_proj_kernel, mode=mode),
        name="dsa_proj_" + mode,
        grid=(ncols // tn, m // tm),
        in_specs=in_specs,
        out_specs=out_specs,
        out_shape=out_shape,
        compiler_params=_params("arbitrary", "arbitrary"),
    )(*args)


def _indexer_kernel(kiab_ref, qi_ref, wt_ref, o_ref, keys_ref, hi_ref, lo_ref, *, tq, topk, seq):
    qb = pl.program_id(1)
    n_kb = qb + 1
    t_idx = qb * tq + lax.broadcasted_iota(I32, (1, tq), 1)
    q_chunk = t_idx // CHUNK

    def key_rows(kb):
        r0 = pl.multiple_of(kb * tq, tq)
        return r0, r0 + lax.broadcasted_iota(I32, (tq, tq), 0)

    def score_body(kb, carry):
        r0, s_idx = key_rows(kb)
        ka = kiab_ref[pl.ds(r0, tq), 0:LANES]
        kb_ = kiab_ref[pl.ds(r0, tq), LANES:2 * LANES]
        acc = jnp.zeros((tq, tq), F32)
        for j in range(IDX_WIDTH // LANES):
            qt = qi_ref[:, j * LANES:(j + 1) * LANES]
            w0 = wt_ref[IDX_DIM + 2 * j:IDX_DIM + 2 * j + 1, :]
            w1 = wt_ref[IDX_DIM + 2 * j + 1:IDX_DIM + 2 * j + 2, :]
            acc = acc + jnp.maximum(_dot_nt(ka, qt), 0.0) * w0 + jnp.maximum(_dot_nt(kb_, qt), 0.0) * w1
        sc = jnp.where((s_idx // CHUNK) <= q_chunk, acc, MASK_VALUE)
        bits = lax.bitcast_convert_type(sc, I32)
        key = bits ^ ((bits >> 31) & 0x7FFFFFFF)
        keys_ref[pl.ds(r0, tq), :] = key
        hi_ref[pl.ds(r0, tq), :] = (key >> 16).astype(I16)
        return carry

    lax.fori_loop(0, n_kb, score_body, 0)

    def count(pred):
        def body(kb, c8):
            r0, s_idx = key_rows(kb)
            p = pred(keys_ref[pl.ds(r0, tq), :], s_idx)
            return c8 + jnp.sum(jnp.where(p, 1, 0).astype(I32).reshape(tq // 8, 8, tq), axis=0)
        c8 = lax.fori_loop(0, n_kb, body, jnp.zeros((8, tq), I32))
        return jnp.sum(c8, axis=0, keepdims=True)

    def count16(ref, pred):
        def body(kb, c16):
            r0 = pl.multiple_of(kb * tq, tq)
            one = jnp.where(pred(ref[pl.ds(r0, tq), :]), jnp.int16(1), jnp.int16(0))
            for r in range(tq // 16):
                c16 = c16 + one[r * 16:(r + 1) * 16, :]
            return c16
        c16 = lax.fori_loop(0, n_kb, body, jnp.zeros((16, tq), I16))
        return jnp.sum(c16.astype(I32), axis=0, keepdims=True)

    def bisect16(ref, rank):
        def bit_body(i, v):
            cand = jnp.where(i == 0, 0, v | lax.shift_left(jnp.int32(1), 15 - i))
            cand16 = cand.astype(I16)
            c = count16(ref, lambda x: x >= cand16)
            return jnp.where(c >= rank, cand, v)
        return lax.fori_loop(0, 16, bit_body, jnp.full((1, tq), -2 ** 15, I32))

    hi = bisect16(hi_ref, topk)
    hi16 = hi.astype(I16)
    rank_lo = topk - count16(hi_ref, lambda x: x > hi16)

    def low_body(kb, carry):
        r0 = pl.multiple_of(kb * tq, tq)
        x = keys_ref[pl.ds(r0, tq), :]
        lo = jnp.where((x >> 16) == hi, (x & 0xFFFF) - 2 ** 15, -2 ** 15)
        lo_ref[pl.ds(r0, tq), :] = lo.astype(I16)
        return carry

    lax.fori_loop(0, n_kb, low_body, 0)
    lo = bisect16(lo_ref, rank_lo)
    thr = (hi << 16) | (lo + 2 ** 15)
    c_ge = count(lambda x, s_idx: x >= thr)
    tie = (c_ge > topk) & (thr > KEY_MASK)
    any_tie = jnp.max(jnp.where(tie, 1, 0)) > 0

    @pl.when(any_tie)
    def _():
        need = topk - count(lambda x, s_idx: x > thr)

        def jbit_body(i, jj):
            cand = jj | lax.shift_left(jnp.int32(1), (seq - 1).bit_length() - 1 - i)
            c = count(lambda x, s_idx: (x == thr) & (s_idx < cand))
            return jnp.where(c < need, cand, jj)

        jj = lax.fori_loop(0, (seq - 1).bit_length(), jbit_body, jnp.zeros((1, tq), I32))
        j_lim = jnp.where(tie, jj, seq)

        def out_body(kb, carry):
            r0, s_idx = key_rows(kb)
            x = keys_ref[pl.ds(r0, tq), :]
            sel = ((x > thr) | ((x == thr) & (s_idx <= j_lim))) & ((s_idx // CHUNK) <= q_chunk)
            o_ref[pl.ds(r0, tq), :] = jnp.where(sel, 0.0, MASK_VALUE).astype(o_ref.dtype)
            return carry

        lax.fori_loop(0, n_kb, out_body, 0)

    @pl.when(jnp.logical_not(any_tie))
    def _():
        def out_body(kb, carry):
            r0 = pl.multiple_of(kb * tq, tq)
            x = keys_ref[pl.ds(r0, tq), :]
            o_ref[pl.ds(r0, tq), :] = jnp.where(x >= thr, 0.0, MASK_VALUE).astype(o_ref.dtype)
            return carry

        lax.fori_loop(0, qb, out_body, 0)
        r0, s_idx = key_rows(qb)
        x = keys_ref[pl.ds(r0, tq), :]
        sel = jnp.where((s_idx // CHUNK) <= q_chunk, jnp.where(x >= thr, 0.0, MASK_VALUE), MASK_VALUE)
        o_ref[pl.ds(r0, tq), :] = sel.astype(o_ref.dtype)

    def fill_body(kb, carry):
        r0 = pl.multiple_of(kb * tq, tq)
        o_ref[pl.ds(r0, tq), :] = jnp.full((tq, tq), MASK_VALUE, o_ref.dtype)
        return carry

    lax.fori_loop(n_kb, seq // tq, fill_body, 0)


def _indexer(kiab, qir, wt, bsz, seq, tq, topk):
    nq = seq // tq
    return pl.pallas_call(
        functools.partial(_indexer_kernel, tq=tq, topk=topk, seq=seq),
        name="dsa_indexer",
        grid=(bsz, nq),
        in_specs=[pl.BlockSpec((seq, 2 * LANES), lambda b, q: (b, 0)),
                  pl.BlockSpec((tq, IDX_WIDTH), lambda b, q: (b * nq + q, 0)),
                  pl.BlockSpec((None, LANES, tq), lambda b, q: (b, 0, q))],
        out_specs=pl.BlockSpec((None, seq, tq), lambda b, q: (b, 0, q)),
        out_shape=jax.ShapeDtypeStruct((bsz, seq, seq), MXU_DTYPE),
        scratch_shapes=[pltpu.VMEM((seq, tq), I32), pltpu.VMEM((seq, tq), I16), pltpu.VMEM((seq, tq), I16)],
        compiler_params=_params("arbitrary", "arbitrary"),
    )(kiab, qir, wt)


def _attn_kernel(q_ref, k_ref, vt_ref, bias_ref, o_ref, m_ref, l_ref, acc_ref, lg_ref, *, tq, tk):
    qb = pl.program_id(1)
    n_kb = ((qb + 1) * tq + tk - 1) // tk
    m_ref[...] = jnp.full(m_ref.shape, -jnp.inf, F32)
    l_ref[...] = jnp.zeros(l_ref.shape, F32)
    acc_ref[...] = jnp.zeros(acc_ref.shape, F32)

    def body(kb, carry):
        r0 = pl.multiple_of(kb * tk, tk)
        bias = bias_ref[pl.ds(r0, tk), :].astype(F32)
        for h in range(B_HEADS):
            hs = slice(h * B_HEAD_DIM, (h + 1) * B_HEAD_DIM)
            lg_ref[h] = _dot_nt(k_ref[pl.ds(r0, tk), hs], q_ref[:, hs]) + bias
        for h in range(B_HEADS):
            hs = slice(h * B_HEAD_DIM, (h + 1) * B_HEAD_DIM)
            lg = lg_ref[h]
            m_i = m_ref[h]
            m_new = jnp.maximum(m_i, jnp.max(lg, axis=0, keepdims=True))
            alpha = jnp.exp2(m_i - m_new)
            p = jnp.exp2(lg - m_new)
            l_ref[h] = alpha * l_ref[h] + jnp.sum(p, axis=0, keepdims=True)
            m_ref[h] = m_new
            acc_ref[h] = alpha * acc_ref[h] + _dot(vt_ref[hs, pl.ds(r0, tk)], p.astype(MXU_DTYPE))
        return carry

    lax.fori_loop(0, n_kb, body, 0)
    for h in range(B_HEADS):
        hs = slice(h * B_HEAD_DIM, (h + 1) * B_HEAD_DIM)
        o_ref[:, hs] = (acc_ref[h] / l_ref[h]).T.astype(o_ref.dtype)


def _attention(qk, vt, bias, bsz, seq, tq):
    nq = seq // tq
    m = bsz * seq
    tk = 2 * tq if seq % (2 * tq) == 0 else tq
    return pl.pallas_call(
        functools.partial(_attn_kernel, tq=tq, tk=tk),
        name="dsa_attention",
        grid=(bsz, nq),
        in_specs=[pl.BlockSpec((tq, B_WIDTH), lambda b, q: (b * nq + q, 0)),
                  _resident((seq, B_WIDTH), lambda b, q: (b, 1)),
                  _resident((None, B_WIDTH, seq), lambda b, q: (b, 0, 0)),
                  pl.BlockSpec((None, seq, tq), lambda b, q: (b, 0, q))],
        out_specs=pl.BlockSpec((tq, B_WIDTH), lambda b, q: (b * nq + q, 0)),
        out_shape=jax.ShapeDtypeStruct((m, B_WIDTH), MXU_DTYPE),
        scratch_shapes=[pltpu.VMEM((B_HEADS, 1, tq), F32), pltpu.VMEM((B_HEADS, 1, tq), F32),
                        pltpu.VMEM((B_HEADS, B_HEAD_DIM, tq), F32),
                        pltpu.VMEM((B_HEADS, tk, tq), F32)],
        compiler_params=_params("arbitrary", "arbitrary"),
    )(qk, qk, vt, bias)


def _mix_kernel(ya_ref, yb_ref, ga_ref, gb_ref, x_ref, wa_ref, wb_ref, wo_ref, g_ref, b_ref,
                xo_ref, xbo_ref, *, alpha):
    merged = ga_ref[...].astype(F32) * _dot(ya_ref[...], wa_ref[...]) \
        + gb_ref[...].astype(F32) * _dot(yb_ref[...], wb_ref[...])
    z = alpha * x_ref[...] + _dot(merged.astype(MXU_DTYPE), wo_ref[...])
    xn = _layer_norm(z, g_ref[...], b_ref[...])
    xo_ref[...] = xn
    xbo_ref[...] = xn.astype(xbo_ref.dtype)


def _mix(ya, yb, gates, x, wa, wb, wo, g, b, layer, alpha, tm):
    m, d = x.shape
    row = lambda w: pl.BlockSpec((tm, w), lambda i: (i, 0))
    vec = pl.BlockSpec((None, 1, d), lambda i: (layer, 0, 0))
    return pl.pallas_call(
        functools.partial(_mix_kernel, alpha=alpha),
        name="mix_out_ln",
        grid=(m // tm,),
        in_specs=[row(A_WIDTH), row(B_WIDTH),
                  pl.BlockSpec((tm, d), lambda i: (i, 0)), pl.BlockSpec((tm, d), lambda i: (i, 1)),
                  row(d),
                  _resident((None, A_WIDTH, d), lambda i: (layer, 0, 0)),
                  _resident((None, B_WIDTH, d), lambda i: (layer, 0, 0)),
                  _resident((None, d, d), lambda i: (layer, 0, 0)),
                  vec, vec],
        out_specs=[row(d), row(d)],
        out_shape=[jax.ShapeDtypeStruct((m, d), F32), jax.ShapeDtypeStruct((m, d), MXU_DTYPE)],
        compiler_params=_params("arbitrary"),
    )(ya, yb, gates, gates, x, wa, wb, wo, g, b)


def _ffn_down_kernel(h_ref, x_ref, wd_ref, g_ref, b_ref, xo_ref, xbo_ref, *, alpha):
    z = alpha * x_ref[...] + _dot(h_ref[...], wd_ref[...])
    xn = _layer_norm(z, g_ref[...], b_ref[...])
    xo_ref[...] = xn
    xbo_ref[...] = xn.astype(xbo_ref.dtype)


def _ffn_down(hid, x, wd, g, b, layer, alpha, tm):
    m, d = x.shape
    d_ff = hid.shape[-1]
    row = pl.BlockSpec((tm, d), lambda i: (i, 0))
    vec = pl.BlockSpec((None, 1, d), lambda i: (layer, 0, 0))
    return pl.pallas_call(
        functools.partial(_ffn_down_kernel, alpha=alpha),
        name="ffn_down_ln",
        grid=(m // tm,),
        in_specs=[pl.BlockSpec((tm, d_ff), lambda i: (i, 0)), row,
                  _resident((None, d_ff, d), lambda i: (layer, 0, 0)), vec, vec],
        out_specs=[row, row],
        out_shape=[jax.ShapeDtypeStruct((m, d), F32), jax.ShapeDtypeStruct((m, d), MXU_DTYPE)],
        compiler_params=_params("arbitrary"),
    )(hid, x, wd, g, b)


def _ple_kernel(xb_ref, x_ref, p_ref, wg_ref, wp_ref, g_ref, b_ref, xo_ref, xbo_ref, *, alpha):
    ple = _sigmoid(_dot(xb_ref[...], wg_ref[...])) * _dot(p_ref[...].astype(MXU_DTYPE), wp_ref[...])
    xn = _layer_norm(alpha * x_ref[...] + ple, g_ref[...], b_ref[...])
    xo_ref[...] = xn
    xbo_ref[...] = xn.astype(xbo_ref.dtype)


def _ple(xb, x, p, wg, wp, g, b, layer, alpha, tm):
    m, d = x.shape
    pd = p.shape[-1]
    row = pl.BlockSpec((tm, d), lambda i: (i, 0))
    vec = pl.BlockSpec((None, 1, d), lambda i: (layer, 0, 0))
    return pl.pallas_call(
        functools.partial(_ple_kernel, alpha=alpha),
        name="ple_ln",
        grid=(m // tm,),
        in_specs=[row, row,
                  pl.BlockSpec((None, tm, pd), lambda i: (layer, i, 0)),
                  _resident((None, d, d), lambda i: (layer, 0, 0)),
                  _resident((None, pd, d), lambda i: (layer, 0, 0)),
                  vec, vec],
        out_specs=[row, row],
        out_shape=[jax.ShapeDtypeStruct((m, d), F32), jax.ShapeDtypeStruct((m, d), MXU_DTYPE)],
        compiler_params=_params("arbitrary"),
    )(xb, x, p, wg, wp, g, b)


def kernel(x, p, positions, w_in, w_branch_a, w_branch_b, w_out, hgrn_lower_bounds, hgrn_norm_g,
           idx_k_norm_g, idx_k_norm_b, ln_mix_g, ln_mix_b, w_ffn_gate, w_ffn_up, w_ffn_down,
           ln_ffn_g, ln_ffn_b, w_ple_gate, w_ple_proj, ln_ple_g, ln_ple_b):
    bsz, seq, d = x.shape
    depth = w_in.shape[0]
    m = bsz * seq
    alpha = (2 * depth) ** 0.25
    topk = min(TOPK_MAX, seq // 4)
    tq = 256
    assert seq % tq == 0 and topk <= tq and tq % CHUNK == 0
    tm = min(1024, m)
    ts = min(512, seq)
    gate_col0 = 4 * A_WIDTH + 3 * B_WIDTH + IDX_WIDTH + IDX_DIM + IDX_HEADS
    assert w_in.shape[-1] == gate_col0 + 2 * d

    cast = lambda w: w.astype(MXU_DTYPE)
    dsa_col0 = 4 * A_WIDTH
    w_hgrn = cast(w_in[:, :, :dsa_col0])
    w_dsa = cast(w_in[:, :, dsa_col0:dsa_col0 + 3 * B_WIDTH + IDX_WIDTH + LANES])
    w_gates = cast(w_in[:, :, gate_col0:])
    wa, wb, wo = cast(w_branch_a), cast(w_branch_b), cast(w_out)
    wg, wu, wd = w_ffn_gate.astype(F32), w_ffn_up.astype(F32), cast(w_ffn_down)
    wpg, wpp = cast(w_ple_gate), cast(w_ple_proj)
    vec3 = lambda v: v.astype(F32).reshape(depth, 1, -1)
    pad_idx = lambda v: jnp.pad(v.astype(F32), ((0, 0), (0, LANES - IDX_DIM))).reshape(depth, 1, LANES)
    kn_g, kn_b = pad_idx(idx_k_norm_g), pad_idx(idx_k_norm_b)
    norm_g = vec3(hgrn_norm_g)
    lbraw = hgrn_lower_bounds.astype(F32)
    ln = [(vec3(g_), vec3(b_)) for g_, b_ in
          ((ln_mix_g, ln_mix_b), (ln_ffn_g, ln_ffn_b), (ln_ple_g, ln_ple_b))]
    p2d = p.reshape(depth, m, p.shape[-1])

    cos_t, sin_t = _rope_tables(positions, ts)
    xf = x.reshape(m, d).astype(F32)
    xb = xf.astype(MXU_DTYPE)
    for layer in range(depth):
        p1 = _matmul(xb, w_hgrn, layer, 0, 4 * A_WIDTH, tm=tm, tn=1024, out_dtype=F32)
        dsa = [_dsa_proj(xb, w_dsa, cos_t, sin_t, kn_g, kn_b, layer, bsz, seq, min(tm, seq), mode)
               for mode in ("qk", "v", "iq", "ikw")]
        qk, vt, qir, (kiab, wt) = dsa
        gates = _matmul(xb, w_gates, layer, 0, 2 * d, tm=tm, tn=1024, out_dtype=MXU_DTYPE, act="sigmoid")
        ya = _hgrn(p1, lbraw, norm_g, layer, bsz, seq, min(256, seq))
        bias = _indexer(kiab, qir, wt, bsz, seq, tq, topk)
        yb = _attention(qk, vt, bias, bsz, seq, tq)
        xf, xb = _mix(ya, yb, gates, xf, wa, wb, wo, ln[0][0], ln[0][1], layer, alpha, min(256, m))
        hid = _swiglu(xb, wg, wu, layer, tm=tm, tn=512)
        xf, xb = _ffn_down(hid, xf, wd, ln[1][0], ln[1][1], layer, alpha, min(256, m))
        xf, xb = _ple(xb, xf, p2d, wpg, wpp, ln[2][0], ln[2][1], layer, alpha, min(256, m))
    return xf.reshape(bsz, seq, d).astype(x.dtype)
```
